```python
import jax, jax.numpy as jnp
from jax import lax
import numpy as np

D_MODEL = 1024
BATCH = 16
SEQ = 256
DEPTH = 4
DEC_BATCH = 8
DEC_SEQ = 2048
PAST_LEN = 512

GRID_W = 64
FFT_GROUPS = 4
FFT_GC = 128
FFT_W = FFT_GROUPS * FFT_GC
SGU_GROUPS = 4
SGU_GC = 128
SGU_W = SGU_GROUPS * SGU_GC
SGU_CHUNK = 128
GLA_H = 4
GLA_DK = 128
GLA_DV = 256
GLA_LR = 16
GLA_GATE_NORM = 16.0
GLA_CHUNK = 64
D_FF = 2816
N_BRANCH = 3
N_ADA = 6
LN_EPS = 1e-5
SPLITS = [FFT_W, SGU_W, SGU_W, GLA_H * GLA_DK, GLA_H * GLA_DK, GLA_H * GLA_DV, GLA_H * GLA_DV, 2 * GLA_LR, N_BRANCH * D_MODEL]
D_IN = sum(SPLITS)

kernel_name = "hybrid_fourier_sgu_gla_diffusion_step"


def _norm32(x):
    x32 = x.astype(jnp.float32)
    mu = jnp.mean(x32, axis=-1, keepdims=True)
    var = jnp.mean(jnp.square(x32 - mu), axis=-1, keepdims=True)
    return (x32 - mu) * lax.rsqrt(var + LN_EPS)


def _layer_norm(x, g, b):
    return (_norm32(x) * g.astype(jnp.float32) + b.astype(jnp.float32)).astype(x.dtype)


def _gla_scan(q, k, v, logg, s0):
    B, T, H, K = q.shape
    V = v.shape[-1]
    L = GLA_CHUNK
    n = T // L
    def r(a):
        return a.reshape(B, n, L, H, a.shape[-1]).transpose(1, 0, 3, 2, 4)
    q, k, v, logg = r(q), r(k), r(v), r(logg)
    G = jnp.cumsum(logg, axis=3)
    G_last = G[:, :, :, -1:, :]
    qg = q * jnp.exp(G)
    kg = k * jnp.exp(-G)
    A = jnp.einsum('nbhlk,nbhmk->nbhlm', qg, kg)
    mask = jnp.tril(jnp.ones((L, L), dtype=bool))
    A = jnp.where(mask, A, 0.0)
    intra = jnp.einsum('nbhlm,nbhmv->nbhlv', A, v)
    dS = jnp.einsum('nbhlk,nbhlv->nbhkv', k * jnp.exp(G_last - G), v)
    decay = jnp.exp(G_last[:, :, :, 0, :])

    def step(S, inp):
        d, ds = inp
        return d[..., None] * S + ds, S

    S_fin, S_start = lax.scan(step, s0, (decay, dS))
    inter = jnp.einsum('nbhlk,nbhkv->nbhlv', qg, S_start)
    o = (intra + inter).transpose(1, 0, 3, 2, 4).reshape(B, T, H, V)
    return o, S_fin


def _mixer(h, s0_f, s0_b, w_in, b_merge, w_fft_out, sgu_g, sgu_ws, sgu_b, w_sgu_out,
           gla_w2, gla_b, gla_norm_g, w_gla_out, w_o):
    B, T, D = h.shape
    idx = np.cumsum(SPLITS)[:-1].tolist()
    z = h @ w_in
    a_f, u, v, q, k, vv, r, lr, gm = jnp.split(z, idx, axis=-1)
    af = a_f.astype(jnp.float32).reshape(B, T, FFT_GROUPS, FFT_GC)
    af = jnp.fft.fft2(af, axes=(1, 3), norm="ortho").real.astype(h.dtype).reshape(B, T, FFT_W)
    y_a = af @ w_fft_out
    u = jax.nn.gelu(u)
    v = jax.nn.gelu(v)
    v = (_norm32(v) * sgu_g.astype(jnp.float32)).astype(h.dtype)
    vc = v.reshape(B, T // SGU_CHUNK, SGU_CHUNK, SGU_GROUPS, SGU_GC)
    vm = jnp.einsum('gpq,bnqgc->bnpgc', sgu_ws, vc) + sgu_b.T[:, :, None]
    y_b = (u * vm.reshape(B, T, SGU_W)) @ w_sgu_out
    qh = q.astype(jnp.float32).reshape(B, T, GLA_H, GLA_DK) * (GLA_DK ** -0.5)
    kh = k.astype(jnp.float32).reshape(B, T, GLA_H, GLA_DK)
    vh = vv.astype(jnp.float32).reshape(B, T, GLA_H, GLA_DV)
    lr2 = lr.astype(jnp.float32).reshape(B, T, 2, GLA_LR)
    zg = jnp.einsum('btdr,drk->btdk', lr2, gla_w2.astype(jnp.float32)) + gla_b.astype(jnp.float32)
    logg = (jax.nn.log_sigmoid(zg) / GLA_GATE_NORM).reshape(B, T, 2, GLA_H, GLA_DK)
    o_f, s_f = _gla_scan(qh, kh, vh, logg[:, :, 0], s0_f)
    o_b, s_b = _gla_scan(qh[:, ::-1], kh[:, ::-1], vh[:, ::-1], logg[:, ::-1, 1], s0_b)
    o = o_f + o_b[:, ::-1]
    o = o * lax.rsqrt(jnp.mean(jnp.square(o), axis=-1, keepdims=True) + LN_EPS) * gla_norm_g.astype(jnp.float32)
    o = o.reshape(B, T, GLA_H * GLA_DV).astype(h.dtype) * jax.nn.silu(r)
    y_c = o @ w_gla_out
    g = jax.nn.sigmoid(gm.astype(jnp.float32).reshape(B, T, N_BRANCH, D) + b_merge.astype(jnp.float32)).astype(h.dtype)
    m = g[:, :, 0] * y_a + g[:, :, 1] * y_b + g[:, :, 2] * y_c
    return m @ w_o, s_f, s_b


def _conv_ffn(h, w_up, w_dw, b_dw, w_down, latent):
    B, T, _ = h.shape
    a, g = jnp.split(h @ w_up, 2, axis=-1)
    if latent:
        rows = T // GRID_W
        ag = a.reshape(B, rows, GRID_W, D_FF)
        conv = lax.conv_general_dilated(ag, w_dw[:, :, None, :].astype(a.dtype), (1, 1), 'SAME',
                                        dimension_numbers=('NHWC', 'HWIO', 'NHWC'),
                                        feature_group_count=D_FF).reshape(B, T, D_FF)
    else:
        w1 = w_dw[1].astype(a.dtype)
        ap = jnp.pad(a, ((0, 0), (1, 1), (0, 0)))
        conv = ap[:, :-2] * w1[0] + ap[:, 1:-1] * w1[1] + ap[:, 2:] * w1[2]
    conv = conv + b_dw
    return (jax.nn.gelu(conv) * g) @ w_down


def _layer(x, cond, latent, s0_f, s0_b, alpha, p):
    mod = (jax.nn.silu(cond.astype(jnp.float32)) @ p['w_ada'].astype(jnp.float32) + p['b_ada'].astype(jnp.float32))
    mod = mod.reshape(cond.shape[0], 1, N_ADA, D_MODEL).astype(x.dtype)
    sh1, sc1, g1, sh2, sc2, g2 = [mod[:, :, i] for i in range(N_ADA)]
    h = (_norm32(x).astype(x.dtype) * (1.0 + sc1) + sh1)
    mix, s_f, s_b = _mixer(h, s0_f, s0_b, p['w_in'], p['b_merge'], p['w_fft_out'], p['sgu_g'], p['sgu_ws'],
                           p['sgu_b'], p['w_sgu_out'], p['gla_w2'], p['gla_b'], p['gla_norm_g'],
                           p['w_gla_out'], p['w_o'])
    x = _layer_norm(alpha * x + g1 * mix.astype(x.dtype), p['ln1_g'], p['ln1_b'])
    h2 = (_norm32(x).astype(x.dtype) * (1.0 + sc2) + sh2)
    f = _conv_ffn(h2, p['w_up'], p['w_dw'], p['b_dw'], p['w_down'], latent)
    x = _layer_norm(alpha * x + g2 * f.astype(x.dtype), p['ln2_g'], p['ln2_b'])
    return x, s_f, s_b


def setup_inputs(seed: int = 0) -> dict:
    key = jax.random.key(seed)
    ks = jax.random.split(key, 32)
    beta = (8.0 * DEPTH) ** -0.25
    f32 = jnp.float32
    nrm = lambda k, shape, s: jax.random.normal(k, shape, f32) * s
    D = D_MODEL
    return {
        "x_prompt": nrm(ks[0], (BATCH, SEQ, D), 1.0),
        "x_sample": nrm(ks[1], (DEC_BATCH, DEC_SEQ, D), 1.0),
        "state_gla": nrm(ks[2], (DEC_BATCH, DEPTH, 2, GLA_H, GLA_DK, GLA_DV), 0.5),
        "c": nrm(ks[3], (DEC_BATCH, D), 1.0),
        "c_ctx": nrm(ks[4], (D,), 1.0),
        "w_in": nrm(ks[5], (DEPTH, D, D_IN), D ** -0.5),
        "b_merge": nrm(ks[6], (DEPTH, N_BRANCH, D), 0.02),
        "w_fft_out": nrm(ks[7], (DEPTH, FFT_W, D), FFT_W ** -0.5),
        "sgu_g": 1.0 + nrm(ks[8], (DEPTH, SGU_W), 0.02),
        "sgu_ws": nrm(ks[9], (DEPTH, SGU_GROUPS, SGU_CHUNK, SGU_CHUNK), SGU_CHUNK ** -0.5),
        "sgu_b": 1.0 + nrm(ks[10], (DEPTH, SGU_GROUPS, SGU_CHUNK), 0.02),
        "w_sgu_out": nrm(ks[11], (DEPTH, SGU_W, D), SGU_W ** -0.5),
        "gla_w2": nrm(ks[12], (DEPTH, 2, GLA_LR, GLA_H * GLA_DK), GLA_LR ** -0.5),
        "gla_b": nrm(ks[13], (DEPTH, 2, GLA_H * GLA_DK), 0.02),
        "gla_norm_g": 1.0 + nrm(ks[14], (DEPTH, GLA_DV), 0.02),
        "w_gla_out": nrm(ks[15], (DEPTH, GLA_H * GLA_DV, D), (GLA_H * GLA_DV) ** -0.5),
        "w_o": nrm(ks[16], (DEPTH, D, D), beta * D ** -0.5),
        "ln1_g": 1.0 + nrm(ks[17], (DEPTH, D), 0.02),
        "ln1_b": nrm(ks[18], (DEPTH, D), 0.02),
        "w_ada": nrm(ks[19], (DEPTH, D, N_ADA * D), 0.5 * D ** -0.5),
        "b_ada": nrm(ks[20], (DEPTH, N_ADA * D), 0.02),
        "w_up": nrm(ks[21], (DEPTH, D, 2 * D_FF), D ** -0.5),
        "w_dw": nrm(ks[22], (DEPTH, 3, 3, D_FF), 1.0 / 3.0),
        "b_dw": nrm(ks[23], (DEPTH, D_FF), 0.02),
        "w_down": nrm(ks[24], (DEPTH, D_FF, D), beta * D_FF ** -0.5),
        "ln2_g": 1.0 + nrm(ks[25], (DEPTH, D), 0.02),
        "ln2_b": nrm(ks[26], (DEPTH, D), 0.02),
    }


def reference(x_prompt, x_sample, state_gla, c, c_ctx, w_in, b_merge, w_fft_out, sgu_g, sgu_ws, sgu_b,
              w_sgu_out, gla_w2, gla_b, gla_norm_g, w_gla_out, w_o, ln1_g, ln1_b, w_ada, b_ada,
              w_up, w_dw, b_dw, w_down, ln2_g, ln2_b):
    alpha = (2.0 * DEPTH) ** 0.25
    xp = x_prompt
    xs = x_sample
    zeros = jnp.zeros((x_prompt.shape[0], GLA_H, GLA_DK, GLA_DV), jnp.float32)
    cond_ctx = c_ctx[None, :]
    new_states = []
    for l in range(DEPTH):
        p = {'w_in': w_in[l], 'b_merge': b_merge[l], 'w_fft_out': w_fft_out[l], 'sgu_g': sgu_g[l],
             'sgu_ws': sgu_ws[l], 'sgu_b': sgu_b[l], 'w_sgu_out': w_sgu_out[l], 'gla_w2': gla_w2[l],
             'gla_b': gla_b[l], 'gla_norm_g': gla_norm_g[l], 'w_gla_out': w_gla_out[l], 'w_o': w_o[l],
             'ln1_g': ln1_g[l], 'ln1_b': ln1_b[l], 'w_ada': w_ada[l], 'b_ada': b_ada[l], 'w_up': w_up[l],
             'w_dw': w_dw[l], 'b_dw': b_dw[l], 'w_down': w_down[l], 'ln2_g': ln2_g[l], 'ln2_b': ln2_b[l]}
        xp, s_f, s_b = _layer(xp, cond_ctx, False, zeros, zeros, alpha, p)
        new_states.append(jnp.stack([s_f, s_b], axis=1))
        xs, _, _ = _layer(xs, c, True, state_gla[:, l, 0].astype(jnp.float32),
                          state_gla[:, l, 1].astype(jnp.float32), alpha, p)
    new_state_gla = jnp.stack(new_states, axis=1).astype(x_prompt.dtype)
    y_prompt = xp.astype(x_prompt.dtype)
    y_sample = xs.astype(x_sample.dtype)
    return (y_prompt, y_sample, new_state_gla)
```

```python
import functools
import math

import numpy as np
import jax
import jax.numpy as jnp
from jax import lax
from jax.experimental import pallas as pl
from jax.experimental.pallas import tpu as pltpu

F32 = jnp.float32
BF16 = jnp.bfloat16

D_MODEL = 1024
GRID_W = 64
FFT_GROUPS = 4
FFT_GC = 128
FFT_W = FFT_GROUPS * FFT_GC
SGU_GROUPS = 4
SGU_GC = 128
SGU_W = SGU_GROUPS * SGU_GC
SGU_CHUNK = 128
GLA_H = 4
GLA_DK = 128
GLA_DV = 256
GLA_LR = 16
GLA_GATE_NORM = 16.0
GLA_CHUNK = 64
GLA_GROUP_ROWS = 256
D_FF = 2816
N_BRANCH = 3
N_ADA = 6
LN_EPS = 1e-5
LR_PAD = 128

COL_VV = 0
COL_R = COL_VV + GLA_H * GLA_DV
COL_GM = COL_R + GLA_H * GLA_DV
COL_AF = COL_GM + N_BRANCH * D_MODEL
COL_U = COL_AF + FFT_W
COL_V = COL_U + SGU_W
COL_Q = COL_V + SGU_W
COL_K = COL_Q + GLA_H * GLA_DK
Z_W = COL_K + GLA_H * GLA_DK

V7X_VMEM_BYTES = 64 * 1024 * 1024
VMEM_LIMIT = V7X_VMEM_BYTES - 8 * 1024 * 1024


def _mm(a, b):
    return jnp.dot(a, b, preferred_element_type=F32)


def _sigmoid(x):
    return 0.5 * jnp.tanh(0.5 * x) + 0.5


def _silu(x):
    return x * _sigmoid(x)


def _gelu(x):
    c = math.sqrt(2.0 / math.pi)
    return 0.5 * x * (1.0 + jnp.tanh(c * (x + 0.044715 * (x * x * x))))


def _norm(x):
    mu = jnp.mean(x, axis=-1, keepdims=True)
    xc = x - mu
    var = jnp.mean(xc * xc, axis=-1, keepdims=True)
    return xc * lax.rsqrt(var + LN_EPS)


def _params(sem, limit=VMEM_LIMIT):
    return pltpu.CompilerParams(dimension_semantics=sem, vmem_limit_bytes=limit)


def _ada_kernel(c_ref, w_ref, b_ref, o_ref):
    s = _silu(c_ref[...]).astype(BF16)
    o_ref[...] = _mm(s, w_ref[...].astype(BF16)) + b_ref[...]


def _ada_call(cond, w_ada, b_ada):
    depth, d, n = w_ada.shape
    rows = cond.shape[0]
    tn = 1536
    return pl.pallas_call(
        _ada_kernel,
        grid=(depth, n // tn),
        in_specs=[
            pl.BlockSpec((rows, d), lambda l, j: (0, 0)),
            pl.BlockSpec((None, d, tn), lambda l, j: (l, 0, j)),
            pl.BlockSpec((None, 1, tn), lambda l, j: (l, 0, j)),
        ],
        out_specs=pl.BlockSpec((None, rows, tn), lambda l, j: (l, 0, j)),
        out_shape=jax.ShapeDtypeStruct((depth, rows, n), F32),
        compiler_params=_params(("arbitrary", "arbitrary")),
        name="ada",
    )(cond, w_ada, b_ada.reshape(depth, 1, n))


def _in_kernel(*refs, alpha, has_prev):
    if has_prev:
        (x_ref, f_ref, pmod_ref, pg_ref, pb_ref, mod_ref, w_ref, wlr_ref,
         z_ref, lr_ref, x2_ref, h_ref) = refs
    else:
        x_ref, mod_ref, w_ref, wlr_ref, z_ref, lr_ref, h_ref = refs

    @pl.when(pl.program_id(1) == 0)
    def _():
        x = x_ref[...]
        if has_prev:
            g2 = pmod_ref[5:6, :]
            x = _norm(alpha * x + g2 * f_ref[...]) * pg_ref[...] + pb_ref[...]
            x2_ref[...] = x
        h = (_norm(x) * (1.0 + mod_ref[1:2, :]) + mod_ref[0:1, :]).astype(BF16)
        h_ref[...] = h
        lr_ref[...] = _mm(h, wlr_ref[...])

    z_ref[...] = _mm(h_ref[...], w_ref[...]).astype(BF16)


def _in_call(x, f, mod, layer, mod_row, prev_ln, w_main, w_lr, alpha):
    m, d = x.shape
    tm = 1024
    tn = 1536
    has_prev = f is not None
    row = lambda i, j: (i, 0)
    in_specs = [pl.BlockSpec((tm, d), row)]
    args = [x]
    if has_prev:
        pg, pb = prev_ln
        in_specs += [
            pl.BlockSpec((tm, d), row),
            pl.BlockSpec((None, None, N_ADA, d), lambda i, j: (layer - 1, mod_row(i), 0, 0)),
            pl.BlockSpec((None, 1, d), lambda i, j: (layer - 1, 0, 0)),
            pl.BlockSpec((None, 1, d), lambda i, j: (layer - 1, 0, 0)),
        ]
        args += [f, mod, pg, pb]
    in_specs += [
        pl.BlockSpec((None, None, N_ADA, d), lambda i, j: (layer, mod_row(i), 0, 0)),
        pl.BlockSpec((None, d, tn), lambda i, j: (layer, 0, j)),
        pl.BlockSpec((None, d, LR_PAD), lambda i, j: (layer, 0, 0)),
    ]
    args += [mod, w_main, w_lr]
    out_specs = [pl.BlockSpec((tm, tn), lambda i, j: (i, j)), pl.BlockSpec((tm, LR_PAD), row)]
    out_shape = [jax.ShapeDtypeStruct((m, Z_W), BF16), jax.ShapeDtypeStruct((m, LR_PAD), F32)]
    if has_prev:
        out_specs.append(pl.BlockSpec((tm, d), row))
        out_shape.append(jax.ShapeDtypeStruct((m, d), F32))
    outs = pl.pallas_call(
        functools.partial(_in_kernel, alpha=alpha, has_prev=has_prev),
        grid=(m // tm, Z_W // tn),
        in_specs=in_specs,
        out_specs=out_specs,
        out_shape=out_shape,
        scratch_shapes=[pltpu.VMEM((tm, d), BF16)],
        compiler_params=_params(("arbitrary", "arbitrary")),
        name="in_proj",
    )(*args)
    if has_prev:
        z, lr, x2 = outs
        return x2, z, lr
    z, lr = outs
    return None, z, lr


@functools.lru_cache(maxsize=None)
def _dft_tables(t):
    def cs(n):
        k = np.arange(n, dtype=np.int64)
        ang = (2.0 * np.pi / n) * ((k[:, None] * k[None, :]) % n).astype(np.float64)
        return np.cos(ang), np.sin(ang)

    ct, st = cs(t)
    cc, sc = cs(FFT_GC)
    tab = np.concatenate([ct, -st], axis=1)
    return tab.astype(np.float32), cc.astype(np.float32), sc.astype(np.float32)


def _fft_kernel(a_ref, gm_ref, tab_ref, cc_ref, sc_ref, w_ref, bm_ref, y_ref, y2_ref, *, t):
    @pl.when(pl.program_id(1) == 0)
    def _():
        for g in range(FFT_GROUPS):
            xg = a_ref[:, g * FFT_GC:(g + 1) * FFT_GC]
            y2_ref[0:t, g * FFT_GC:(g + 1) * FFT_GC] = _mm(xg, cc_ref[...]).astype(BF16)
            y2_ref[t:2 * t, g * FFT_GC:(g + 1) * FFT_GC] = _mm(xg, sc_ref[...]).astype(BF16)

    scale = 1.0 / math.sqrt(t * FFT_GC)
    af = (_mm(tab_ref[...], y2_ref[...]) * scale).astype(BF16)
    ya = _mm(af, w_ref[...])
    gate = _sigmoid(gm_ref[...].astype(F32) + bm_ref[...])
    y_ref[...] = (gate * ya).astype(BF16)


def _fft_call(z, layer, b, t, tables, w_fft_out, b_merge):
    m = z.shape[0]
    tr = min(t, 512)
    nr = t // tr
    tab, cc, sc = tables
    return pl.pallas_call(
        functools.partial(_fft_kernel, t=t),
        grid=(b, nr),
        in_specs=[
            pl.BlockSpec((t, FFT_W), lambda i, r: (i, COL_AF // FFT_W)),
            pl.BlockSpec((tr, D_MODEL), lambda i, r: (i * nr + r, COL_GM // D_MODEL)),
            pl.BlockSpec((tr, 2 * t), lambda i, r: (r, 0)),
            pl.BlockSpec((FFT_GC, FFT_GC), lambda i, r: (0, 0)),
            pl.BlockSpec((FFT_GC, FFT_GC), lambda i, r: (0, 0)),
            pl.BlockSpec((None, FFT_W, D_MODEL), lambda i, r: (layer, 0, 0)),
            pl.BlockSpec((None, None, 1, D_MODEL), lambda i, r: (layer, 0, 0, 0)),
        ],
        out_specs=pl.BlockSpec((tr, D_MODEL), lambda i, r: (i * nr + r, 0)),
        out_shape=jax.ShapeDtypeStruct((m, D_MODEL), BF16),
        scratch_shapes=[pltpu.VMEM((2 * t, FFT_W), BF16)],
        compiler_params=_params(("arbitrary", "arbitrary")),
        name="fft",
    )(z, z, tab, cc, sc, w_fft_out, b_merge)


def _sgu_kernel(u_ref, v_ref, gm_ref, g_ref, ws_ref, bias_ref, w_ref, bm_ref, y_ref, vm_ref):
    tm = u_ref.shape[0]
    v = _gelu(v_ref[...].astype(F32))
    vn = (_norm(v) * g_ref[...]).astype(BF16)
    for n in range(tm // SGU_CHUNK):
        rows = slice(n * SGU_CHUNK, (n + 1) * SGU_CHUNK)
        for g in range(SGU_GROUPS):
            cols = slice(g * SGU_GC, (g + 1) * SGU_GC)
            vm_ref[rows, cols] = _mm(ws_ref[g], vn[rows, cols]) + bias_ref[:, cols]
    u = _gelu(u_ref[...].astype(F32))
    yb = _mm((u * vm_ref[...]).astype(BF16), w_ref[...])
    gate = _sigmoid(gm_ref[...].astype(F32) + bm_ref[...])
    y_ref[...] = (gate * yb).astype(BF16)


def _sgu_call(z, layer, sgu_g, sgu_ws, sgu_bias, w_sgu_out, b_merge):
    m = z.shape[0]
    tm = 1024
    return pl.pallas_call(
        _sgu_kernel,
        grid=(m // tm,),
        in_specs=[
            pl.BlockSpec((tm, SGU_W), lambda i: (i, COL_U // SGU_W)),
            pl.BlockSpec((tm, SGU_W), lambda i: (i, COL_V // SGU_W)),
            pl.BlockSpec((tm, D_MODEL), lambda i: (i, COL_GM // D_MODEL + 1)),
            pl.BlockSpec((None, 1, SGU_W), lambda i: (layer, 0, 0)),
            pl.BlockSpec((None, SGU_GROUPS, SGU_CHUNK, SGU_CHUNK), lambda i: (layer, 0, 0, 0)),
            pl.BlockSpec((None, SGU_CHUNK, SGU_W), lambda i: (layer, 0, 0)),
            pl.BlockSpec((None, SGU_W, D_MODEL), lambda i: (layer, 0, 0)),
            pl.BlockSpec((None, None, 1, D_MODEL), lambda i: (layer, 1, 0, 0)),
        ],
        out_specs=pl.BlockSpec((tm, D_MODEL), lambda i: (i, 0)),
        out_shape=jax.ShapeDtypeStruct((m, D_MODEL), BF16),
        scratch_shapes=[pltpu.VMEM((tm, SGU_W), F32)],
        compiler_params=_params(("arbitrary",)),
        name="sgu",
    )(z, z, z, sgu_g, sgu_ws, sgu_bias, w_sgu_out, b_merge)


def _gla_kernel(*refs, t, has_s0, want_state):
    refs = list(refs)
    q_ref, k_ref, v_ref, r_ref, lr_ref, w2_ref, b2_ref = refs[:7]
    del refs[:7]
    s0_ref = refs.pop(0) if has_s0 else None
    ng_ref, wout_ref, gm_ref, bm_ref, y_ref = refs[:5]
    del refs[:5]
    sfin_ref = refs.pop(0) if want_state else None
    oall_ref, o_ref, lg_ref, qg_ref, kg_ref, ds_ref, ss_ref, gtot_ref = refs

    head = pl.program_id(1)
    ch = GLA_CHUNK
    n = t // ch
    gr = GLA_GROUP_ROWS
    cpg = gr // ch
    n_groups = t // gr
    unroll = min(n_groups, 4)
    nt = (((1,), (1,)), ((), ()))
    tn = (((0,), (0,)), ((), ()))

    lrb = lr_ref[...].astype(BF16)
    for d in range(2):
        zg = _mm(lrb, w2_ref[d]) + b2_ref[d]
        lg_ref[d] = (jnp.minimum(zg, 0.0) - jnp.log1p(jnp.exp(-jnp.abs(zg)))) * (1.0 / GLA_GATE_NORM)

    def split3(x):
        hi = x.astype(BF16)
        rem = x - hi.astype(F32)
        mid = rem.astype(BF16)
        lo = (rem - mid.astype(F32)).astype(BF16)
        return hi, mid, lo

    row = lax.broadcasted_iota(jnp.int32, (gr, gr), 0)
    col = lax.broadcasted_iota(jnp.int32, (gr, gr), 1)
    same = (row // ch) == (col // ch)
    incl = (same & (col <= row), same & (col >= row))
    cum_mats = tuple(m.astype(BF16) for m in incl)
    brow = lax.broadcasted_iota(jnp.int32, (gr, cpg * GLA_DK), 0)
    bcol = lax.broadcasted_iota(jnp.int32, (gr, cpg * GLA_DK), 1)
    own_chunk = (brow // ch) == (bcol // GLA_DK)

    def spread(x):
        return jnp.where(own_chunk, jnp.concatenate([x] * cpg, axis=1), jnp.zeros((), x.dtype))

    if gtot_ref.shape[0] > n * 8:
        gtot_ref[...] = jnp.zeros_like(gtot_ref)

    for d in range(2):
        def prepare(g, carry, d=d):
            r0 = pl.multiple_of(g * gr, gr)
            lg3 = jnp.concatenate(split3(lg_ref[d, pl.ds(r0, gr), :]), axis=1)
            cums = _mm(cum_mats[d], lg3)
            gc = cums[:, 0:GLA_DK] + cums[:, GLA_DK:2 * GLA_DK] + cums[:, 2 * GLA_DK:3 * GLA_DK]
            g4 = gc.reshape(cpg, ch, GLA_DK)
            tot = g4[:, ch - 1:ch, :] if d == 0 else g4[:, 0:1, :]
            rc = (tot - g4).reshape(gr, GLA_DK)
            t0 = pl.multiple_of(g * (cpg * 8), cpg * 8)
            gtot_ref[pl.ds(t0, cpg * 8), :] = jnp.broadcast_to(tot, (cpg, 8, GLA_DK)).reshape(cpg * 8, GLA_DK)
            q = q_ref[pl.ds(r0, gr), :].astype(F32) * (GLA_DK ** -0.5)
            k = k_ref[pl.ds(r0, gr), :].astype(F32)
            qg_ref[pl.ds(r0, gr), :] = (q * jnp.exp(gc)).astype(BF16)
            kg_ref[pl.ds(r0, gr), :] = (k * jnp.exp(-gc)).astype(BF16)
            kd = (k * jnp.exp(rc)).astype(BF16)
            d0 = pl.multiple_of(g * (cpg * GLA_DK), cpg * GLA_DK)
            ds_ref[pl.ds(d0, cpg * GLA_DK), :] = lax.dot_general(
                spread(kd), v_ref[pl.ds(r0, gr), :], tn, preferred_element_type=F32)
            return carry

        lax.fori_loop(0, n_groups, prepare, 0, unroll=unroll)

        dec = jnp.exp(gtot_ref[...].T)
        s = s0_ref[d] if has_s0 else jnp.zeros((GLA_DK, GLA_DV), F32)
        for c in (range(n) if d == 0 else range(n - 1, -1, -1)):
            ss_ref[c * GLA_DK:(c + 1) * GLA_DK, :] = s.astype(BF16)
            s = s * dec[:, 8 * c:8 * c + 1] + ds_ref[c * GLA_DK:(c + 1) * GLA_DK, :]
        if want_state:
            sfin_ref[d] = s

        def attend(g, carry, d=d):
            r0 = pl.multiple_of(g * gr, gr)
            d0 = pl.multiple_of(g * (cpg * GLA_DK), cpg * GLA_DK)
            qg = qg_ref[pl.ds(r0, gr), :]
            a = lax.dot_general(qg, kg_ref[pl.ds(r0, gr), :], nt, preferred_element_type=F32)
            a = jnp.where(incl[d], a, 0.0).astype(BF16)
            o = _mm(a, v_ref[pl.ds(r0, gr), :]) + _mm(spread(qg), ss_ref[pl.ds(d0, cpg * GLA_DK), :])
            if d == 0:
                o_ref[pl.ds(r0, gr), :] = o
            else:
                o_ref[pl.ds(r0, gr), :] += o
            return carry

        lax.fori_loop(0, n_groups, attend, 0, unroll=unroll)

    o = o_ref[...]
    o = o * lax.rsqrt(jnp.mean(o * o, axis=-1, keepdims=True) + LN_EPS) * ng_ref[...]
    oall_ref[head] = (o * _silu(r_ref[...].astype(F32))).astype(BF16)

    @pl.when(head == pl.num_programs(1) - 1)
    def _():
        o_heads = jnp.concatenate([oall_ref[h] for h in range(GLA_H)], axis=1)
        gate = _sigmoid(gm_ref[...].astype(F32) + bm_ref[...])
        y_ref[...] = (gate * _mm(o_heads, wout_ref[...])).astype(BF16)


def _gla_call(z, lr, layer, b, t, state, w2h, b2h, gla_norm_g, w_gla_out, b_merge, want_state):
    m = z.shape[0]
    has_s0 = state is not None
    n_chunks = t // GLA_CHUNK
    in_specs = [
        pl.BlockSpec((t, GLA_DK), lambda i, h: (i, COL_Q // GLA_DK + h)),
        pl.BlockSpec((t, GLA_DK), lambda i, h: (i, COL_K // GLA_DK + h)),
        pl.BlockSpec((t, GLA_DV), lambda i, h: (i, COL_VV // GLA_DV + h)),
        pl.BlockSpec((t, GLA_DV), lambda i, h: (i, COL_R // GLA_DV + h)),
        pl.BlockSpec((t, LR_PAD), lambda i, h: (i, 0)),
        pl.BlockSpec((None, None, 2, LR_PAD, GLA_DK), lambda i, h: (layer, h, 0, 0, 0)),
        pl.BlockSpec((None, None, 2, 1, GLA_DK), lambda i, h: (layer, h, 0, 0, 0)),
    ]
    args = [z, z, z, z, lr, w2h, b2h]
    if has_s0:
        in_specs.append(pl.BlockSpec((None, None, 2, None, GLA_DK, GLA_DV), lambda i, h: (i, layer, 0, h, 0, 0)))
        args.append(state)
    in_specs += [
        pl.BlockSpec((None, 1, GLA_DV), lambda i, h: (layer, 0, 0)),
        pl.BlockSpec((None, GLA_H * GLA_DV, D_MODEL), lambda i, h: (layer, 0, 0)),
        pl.BlockSpec((t, D_MODEL), lambda i, h: (i, COL_GM // D_MODEL + 2)),
        pl.BlockSpec((None, None, 1, D_MODEL), lambda i, h: (layer, 2, 0, 0)),
    ]
    args += [gla_norm_g, w_gla_out, z, b_merge]
    out_specs = [pl.BlockSpec((t, D_MODEL), lambda i, h: (i, 0))]
    out_shape = [jax.ShapeDtypeStruct((m, D_MODEL), BF16)]
    if want_state:
        out_specs.append(pl.BlockSpec((None, 2, None, GLA_DK, GLA_DV), lambda i, h: (i, 0, h, 0, 0)))
        out_shape.append(jax.ShapeDtypeStruct((b, 2, GLA_H, GLA_DK, GLA_DV), F32))
    outs = pl.pallas_call(
        functools.partial(_gla_kernel, t=t, has_s0=has_s0, want_state=want_state),
        grid=(b, GLA_H),
        in_specs=in_specs,
        out_specs=out_specs,
        out_shape=out_shape,
        scratch_shapes=[
            pltpu.VMEM((GLA_H, t, GLA_DV), BF16),
            pltpu.VMEM((t, GLA_DV), F32),
            pltpu.VMEM((2, t, GLA_DK), F32),
            pltpu.VMEM((t, GLA_DK), BF16),
            pltpu.VMEM((t, GLA_DK), BF16),
            pltpu.VMEM((n_chunks * GLA_DK, GLA_DV), F32),
            pltpu.VMEM((n_chunks * GLA_DK, GLA_DV), BF16),
            pltpu.VMEM((max(n_chunks * 8, GLA_DK), GLA_DK), F32),
        ],
        compiler_params=_params(("arbitrary", "arbitrary")),
        name="gla",
    )(*args)
    return (outs[0], outs[1]) if want_state else (outs[0], None)


def _mix_kernel(ya_ref, yb_ref, yc_ref, x_ref, mod_ref, wo_ref, g_ref, b_ref, x1_ref, h2_ref, *, alpha):
    msum = (ya_ref[...].astype(F32) + yb_ref[...].astype(F32) + yc_ref[...].astype(F32)).astype(BF16)
    mix = _mm(msum, wo_ref[...])
    x1 = _norm(alpha * x_ref[...] + mod_ref[2:3, :] * mix) * g_ref[...] + b_ref[...]
    x1_ref[...] = x1
    h2_ref[...] = (_norm(x1) * (1.0 + mod_ref[4:5, :]) + mod_ref[3:4, :]).astype(BF16)


def _mix_call(ya, yb, yc, x, mod, layer, mod_row, w_o, ln1_g, ln1_b, alpha):
    m, d = x.shape
    tm = 512
    row = lambda i: (i, 0)
    vec = pl.BlockSpec((None, 1, d), lambda i: (layer, 0, 0))
    return pl.pallas_call(
        functools.partial(_mix_kernel, alpha=alpha),
        grid=(m // tm,),
        in_specs=[
            pl.BlockSpec((tm, d), row), pl.BlockSpec((tm, d), row), pl.BlockSpec((tm, d), row),
            pl.BlockSpec((tm, d), row),
            pl.BlockSpec((None, None, N_ADA, d), lambda i: (layer, mod_row((i * tm) // 1024), 0, 0)),
            pl.BlockSpec((None, d, d), lambda i: (layer, 0, 0)),
            vec, vec,
        ],
        out_specs=[pl.BlockSpec((tm, d), row), pl.BlockSpec((tm, d), row)],
        out_shape=[jax.ShapeDtypeStruct((m, d), F32), jax.ShapeDtypeStruct((m, d), BF16)],
        compiler_params=_params(("arbitrary",)),
        name="mix",
    )(ya, yb, yc, x, mod, w_o, ln1_g, ln1_b)


def _ffn_kernel(h2_ref, wa_ref, wg_ref, wdw_ref, bdw_ref, wd_ref, f_ref, *, seq, latent):
    h2 = h2_ref[...]
    a = _mm(h2, wa_ref[...])
    g = _mm(h2, wg_ref[...])
    rows, tc = a.shape
    period = GRID_W if latent else seq
    a3 = a.reshape(rows // period, period, tc)
    pos = lax.broadcasted_iota(jnp.int32, (1, period, tc), 1)
    left = jnp.where(pos > 0, pltpu.roll(a3, 1, axis=1), 0.0)
    right = jnp.where(pos < period - 1, pltpu.roll(a3, period - 1, axis=1), 0.0)

    def line(dh):
        w = lambda i: wdw_ref[3 * dh + i:3 * dh + i + 1, :].reshape(1, 1, tc)
        return w(0) * left + w(1) * a3 + w(2) * right

    conv = line(1)
    if latent:
        zero = jnp.zeros((1, period, tc), F32)
        conv = conv + jnp.concatenate([zero, line(0)[:-1]], axis=0)
        conv = conv + jnp.concatenate([line(2)[1:], zero], axis=0)
    conv = conv.reshape(rows, tc) + bdw_ref[...]
    act = (_gelu(conv) * g).astype(BF16)
    contrib = _mm(act, wd_ref[...])

    @pl.when(pl.program_id(1) == 0)
    def _():
        f_ref[...] = contrib

    @pl.when(pl.program_id(1) != 0)
    def _():
        f_ref[...] += contrib


def _ffn_call(h2, layer, seq, latent, w_up, w_dw, b_dw, w_down):
    m, d = h2.shape
    rows = 2048
    tc = 256
    nk = D_FF // tc
    if latent:
        assert rows == seq
    else:
        assert rows % seq == 0
    return pl.pallas_call(
        functools.partial(_ffn_kernel, seq=seq, latent=latent),
        grid=(m // rows, nk),
        in_specs=[
            pl.BlockSpec((rows, d), lambda i, k: (i, 0)),
            pl.BlockSpec((None, d, tc), lambda i, k: (layer, 0, k)),
            pl.BlockSpec((None, d, tc), lambda i, k: (layer, 0, nk + k)),
            pl.BlockSpec((None, 9, tc), lambda i, k: (layer, 0, k)),
            pl.BlockSpec((None, 1, tc), lambda i, k: (layer, 0, k)),
            pl.BlockSpec((None, tc, d), lambda i, k: (layer, k, 0)),
        ],
        out_specs=pl.BlockSpec((rows, d), lambda i, k: (i, 0)),
        out_shape=jax.ShapeDtypeStruct((m, d), F32),
        compiler_params=_params(("arbitrary", "arbitrary")),
        name="ffn",
    )(h2, w_up, w_up, w_dw, b_dw, w_down)


def _ln2_kernel(x_ref, f_ref, mod_ref, g_ref, b_ref, o_ref, *, alpha):
    o_ref[...] = _norm(alpha * x_ref[...] + mod_ref[5:6, :] * f_ref[...]) * g_ref[...] + b_ref[...]


def _ln2_call(x1, f, mod, layer, mod_row, ln2_g, ln2_b, alpha):
    m, d = x1.shape
    tm = 1024
    row = lambda i: (i, 0)
    vec = pl.BlockSpec((None, 1, d), lambda i: (layer, 0, 0))
    return pl.pallas_call(
        functools.partial(_ln2_kernel, alpha=alpha),
        grid=(m // tm,),
        in_specs=[
            pl.BlockSpec((tm, d), row), pl.BlockSpec((tm, d), row),
            pl.BlockSpec((None, None, N_ADA, d), lambda i: (layer, mod_row(i), 0, 0)),
            vec, vec,
        ],
        out_specs=pl.BlockSpec((tm, d), row),
        out_shape=jax.ShapeDtypeStruct((m, d), F32),
        compiler_params=_params(("arbitrary",)),
        name="ln2",
    )(x1, f, mod, ln2_g, ln2_b)


def kernel(x_prompt, x_sample, state_gla, c, c_ctx, w_in, b_merge, w_fft_out, sgu_g, sgu_ws, sgu_b, w_sgu_out, gla_w2, gla_b, gla_norm_g, w_gla_out, w_o, ln1_g, ln1_b, w_ada, b_ada, w_up, w_dw, b_dw, w_down, ln2_g, ln2_b):
    depth = w_in.shape[0]
    alpha = (2.0 * depth) ** 0.25
    bp, tp, d = x_prompt.shape
    bs, ts, _ = x_sample.shape

    n_cond = 16
    cond = jnp.zeros((n_cond, d), F32).at[:bs].set(c.astype(F32)).at[bs].set(c_ctx.astype(F32))
    mod = _ada_call(cond, w_ada, b_ada).reshape(depth, n_cond, N_ADA, d)

    cuts = np.cumsum([FFT_W, SGU_W, SGU_W, GLA_H * GLA_DK, GLA_H * GLA_DK, GLA_H * GLA_DV, GLA_H * GLA_DV, 2 * GLA_LR]).tolist()
    p_af, p_u, p_v, p_q, p_k, p_vv, p_r, p_lr, p_gm = jnp.split(w_in, cuts, axis=-1)
    w_main = jnp.concatenate([p_vv, p_r, p_gm, p_af, p_u, p_v, p_q, p_k], axis=-1).astype(BF16)
    w_lr = jnp.pad(p_lr, ((0, 0), (0, 0), (0, LR_PAD - 2 * GLA_LR))).astype(BF16)
    w2 = gla_w2.reshape(depth, 2, GLA_LR, GLA_H, GLA_DK).transpose(0, 3, 1, 2, 4)
    w2h = jnp.zeros((depth, GLA_H, 2, LR_PAD, GLA_DK), F32)
    w2h = w2h.at[:, :, 0, 0:GLA_LR].set(w2[:, :, 0]).at[:, :, 1, GLA_LR:2 * GLA_LR].set(w2[:, :, 1]).astype(BF16)
    b2h = gla_b.reshape(depth, 2, GLA_H, 1, GLA_DK).transpose(0, 2, 1, 3, 4).astype(F32)
    sgu_bias = jnp.repeat(sgu_b.transpose(0, 2, 1), SGU_GC, axis=-1).astype(F32)
    vec = lambda a: a.reshape(depth, 1, a.shape[-1]).astype(F32)
    bm = b_merge.reshape(depth, N_BRANCH, 1, d).astype(F32)
    w_fft_out_b, w_sgu_out_b, sgu_ws_b = w_fft_out.astype(BF16), w_sgu_out.astype(BF16), sgu_ws.astype(BF16)
    w_gla_out_b, w_o_b, w_up_b, w_down_b = w_gla_out.astype(BF16), w_o.astype(BF16), w_up.astype(BF16), w_down.astype(BF16)
    w_dw9 = w_dw.reshape(depth, 9, D_FF).astype(F32)
    sgu_g3, ng3, ln1g, ln1b, ln2g, ln2b, bdw = (vec(a) for a in (sgu_g, gla_norm_g, ln1_g, ln1_b, ln2_g, ln2_b, b_dw))

    streams = [
        dict(x=x_prompt.reshape(bp * tp, d), b=bp, t=tp, latent=False, state=None, mod_row=lambda i: bs),
        dict(x=x_sample.reshape(bs * ts, d), b=bs, t=ts, latent=True, state=state_gla.astype(F32),
             mod_row=lambda i: (i * 1024) // ts),
    ]
    new_states = []
    for s in streams:
        x, f = s["x"].astype(F32), None
        tables = tuple(jnp.asarray(a).astype(BF16) for a in _dft_tables(s["t"]))
        for l in range(depth):
            prev_ln = (ln2g, ln2b) if l else None
            x2, z, lr = _in_call(x, f, mod, l, s["mod_row"], prev_ln, w_main, w_lr, alpha)
            x = x2 if l else x
            ya = _fft_call(z, l, s["b"], s["t"], tables, w_fft_out_b, bm)
            yb = _sgu_call(z, l, sgu_g3, sgu_ws_b, sgu_bias, w_sgu_out_b, bm)
            yc, st = _gla_call(z, lr, l, s["b"], s["t"], s["state"], w2h, b2h, ng3, w_gla_out_b, bm,
                               want_state=not s["latent"])
            if st is not None:
                new_states.append(st)
            x, h2 = _mix_call(ya, yb, yc, x, mod, l, s["mod_row"], w_o_b, ln1g, ln1b, alpha)
            f = _ffn_call(h2, l, s["t"], s["latent"], w_up_b, w_dw9, bdw, w_down_b)
        s["y"] = _ln2_call(x, f, mod, depth - 1, s["mod_row"], ln2g, ln2b, alpha)

    y_prompt = streams[0]["y"].reshape(bp, tp, d).astype(x_prompt.dtype)
    y_sample = streams[1]["y"].reshape(bs, ts, d).astype(x_sample.dtype)
    new_state_gla = jnp.stack(new_states, axis=1).astype(x_prompt.dtype)
    return (y_prompt, y_sample, new_state_gla)
```

```python
import functools
import math

import numpy as np
import jax
import jax.numpy as jnp
from jax import lax
from jax.experimental import pallas as pl
from jax.experimental.pallas import tpu as pltpu

F32 = jnp.float32
BF16 = jnp.bfloat16

D_MODEL = 1024
GRID_W = 64
FFT_GROUPS = 4
FFT_GC = 128
FFT_W = FFT_GROUPS * FFT_GC
SGU_GROUPS = 4
SGU_GC = 128
SGU_W = SGU_GROUPS * SGU_GC
SGU_CHUNK = 128
GLA_H = 4
GLA_DK = 128
GLA_DV = 256
GLA_LR = 16
GLA_GATE_NORM = 16.0
GLA_CHUNK = 64
GLA_GROUP_ROWS = 256
D_FF = 2816
N_BRANCH = 3
N_ADA = 6
LN_EPS = 1e-5
LR_PAD = 128
IN_CHUNK_ROWS = 256
FFN_CHUNK_ROWS = 256

COL_VV = 0
COL_R = COL_VV + GLA_H * GLA_DV
COL_GM = COL_R + GLA_H * GLA_DV
COL_AF = COL_GM + N_BRANCH * D_MODEL
COL_U = COL_AF + FFT_W
COL_V = COL_U + SGU_W
COL_Q = COL_V + SGU_W
COL_K = COL_Q + GLA_H * GLA_DK
Z_W = COL_K + GLA_H * GLA_DK

V7X_VMEM_BYTES = 64 * 1024 * 1024
VMEM_LIMIT = V7X_VMEM_BYTES - 8 * 1024 * 1024


def _mm(a, b):
    return jnp.dot(a, b, preferred_element_type=F32)


def _sigmoid(x):
    return 0.5 * jnp.tanh(0.5 * x) + 0.5


def _silu(x):
    return x * _sigmoid(x)


def _gelu(x):
    c = math.sqrt(2.0 / math.pi)
    half = 0.5 * x
    return half + half * jnp.tanh(x * (c + (c * 0.044715) * (x * x)))


def _norm(x):
    mu = jnp.mean(x, axis=-1, keepdims=True)
    xc = x - mu
    var = jnp.mean(xc * xc, axis=-1, keepdims=True)
    return xc * lax.rsqrt(var + LN_EPS)


def _params(sem, limit=VMEM_LIMIT, flags=None):
    return pltpu.CompilerParams(dimension_semantics=sem, vmem_limit_bytes=limit, flags=flags)


def _ada_kernel(c_ref, w_ref, b_ref, o_ref):
    s = _silu(c_ref[...]).astype(BF16)
    o_ref[...] = _mm(s, w_ref[...].astype(BF16)) + b_ref[...]


def _ada_call(cond, w_ada, b_ada):
    depth, d, n = w_ada.shape
    rows = cond.shape[0]
    tn = 1536
    return pl.pallas_call(
        _ada_kernel,
        grid=(depth, n // tn),
        in_specs=[
            pl.BlockSpec((rows, d), lambda l, j: (0, 0)),
            pl.BlockSpec((None, d, tn), lambda l, j: (l, 0, j)),
            pl.BlockSpec((None, 1, tn), lambda l, j: (l, 0, j)),
        ],
        out_specs=pl.BlockSpec((None, rows, tn), lambda l, j: (l, 0, j)),
        out_shape=jax.ShapeDtypeStruct((depth, rows, n), F32),
        compiler_params=_params(("arbitrary", "arbitrary")),
        name="ada",
    )(cond, w_ada, b_ada.reshape(depth, 1, n))


def _in_kernel(*refs, alpha, has_prev):
    if has_prev:
        (x_ref, f_ref, pmod_ref, pg_ref, pb_ref, mod_ref, w_ref, wlr_ref,
         z_ref, lr_ref, x2_ref, h_ref) = refs
    else:
        x_ref, mod_ref, w_ref, wlr_ref, z_ref, lr_ref, h_ref = refs

    @pl.when(pl.program_id(1) == 0)
    def _():
        cr = IN_CHUNK_ROWS
        for c in range(x_ref.shape[0] // cr):
            rows = slice(c * cr, (c + 1) * cr)
            x = x_ref[rows, :]
            if has_prev:
                g2 = pmod_ref[5:6, :]
                x = _norm(alpha * x + g2 * f_ref[rows, :]) * pg_ref[...] + pb_ref[...]
                x2_ref[rows, :] = x
            h = (_norm(x) * (1.0 + mod_ref[1:2, :]) + mod_ref[0:1, :]).astype(BF16)
            h_ref[rows, :] = h
            lr_ref[rows, :] = _mm(h, wlr_ref[...])
            z_ref[rows, :] = _mm(h, w_ref[...]).astype(BF16)

    @pl.when(pl.program_id(1) != 0)
    def _():
        z_ref[...] = _mm(h_ref[...], w_ref[...]).astype(BF16)


def _in_call(x, f, mod, layer, mod_row, prev_ln, w_main, w_lr, alpha):
    m, d = x.shape
    tm = 1024
    tn = 1536
    has_prev = f is not None
    row = lambda i, j: (i, 0)
    in_specs = [pl.BlockSpec((tm, d), row)]
    args = [x]
    if has_prev:
        pg, pb = prev_ln
        in_specs += [
            pl.BlockSpec((tm, d), row),
            pl.BlockSpec((None, None, N_ADA, d), lambda i, j: (layer - 1, mod_row(i), 0, 0)),
            pl.BlockSpec((None, 1, d), lambda i, j: (layer - 1, 0, 0)),
            pl.BlockSpec((None, 1, d), lambda i, j: (layer - 1, 0, 0)),
        ]
        args += [f, mod, pg, pb]
    in_specs += [
        pl.BlockSpec((None, None, N_ADA, d), lambda i, j: (layer, mod_row(i), 0, 0)),
        pl.BlockSpec((None, d, tn), lambda i, j: (layer, 0, j)),
        pl.BlockSpec((None, d, LR_PAD), lambda i, j: (layer, 0, 0)),
    ]
    args += [mod, w_main, w_lr]
    out_specs = [pl.BlockSpec((tm, tn), lambda i, j: (i, j)), pl.BlockSpec((tm, LR_PAD), row)]
    out_shape = [jax.ShapeDtypeStruct((m, Z_W), BF16), jax.ShapeDtypeStruct((m, LR_PAD), F32)]
    if has_prev:
        out_specs.append(pl.BlockSpec((tm, d), row))
        out_shape.append(jax.ShapeDtypeStruct((m, d), F32))
    outs = pl.pallas_call(
        functools.partial(_in_kernel, alpha=alpha, has_prev=has_prev),
        grid=(m // tm, Z_W // tn),
        in_specs=in_specs,
        out_specs=out_specs,
        out_shape=out_shape,
        scratch_shapes=[pltpu.VMEM((tm, d), BF16)],
        compiler_params=_params(("arbitrary", "arbitrary")),
        name="in_proj",
    )(*args)
    if has_prev:
        z, lr, x2 = outs
        return x2, z, lr
    z, lr = outs
    return None, z, lr


@functools.lru_cache(maxsize=None)
def _dft_tables(t):
    def cs(n):
        k = np.arange(n, dtype=np.int64)
        ang = (2.0 * np.pi / n) * ((k[:, None] * k[None, :]) % n).astype(np.float64)
        return np.cos(ang), np.sin(ang)

    ct, st = cs(t)
    cc, sc = cs(FFT_GC)
    tab = np.concatenate([ct, -st], axis=1)
    return tab.astype(np.float32), cc.astype(np.float32), sc.astype(np.float32)


def _fft_kernel(a_ref, gm_ref, tab_ref, cc_ref, sc_ref, w_ref, bm_ref, y_ref, y2_ref, *, t):
    @pl.when(pl.program_id(1) == 0)
    def _():
        for g in range(FFT_GROUPS):
            xg = a_ref[:, g * FFT_GC:(g + 1) * FFT_GC]
            y2_ref[0:t, g * FFT_GC:(g + 1) * FFT_GC] = _mm(xg, cc_ref[...]).astype(BF16)
            y2_ref[t:2 * t, g * FFT_GC:(g + 1) * FFT_GC] = _mm(xg, sc_ref[...]).astype(BF16)

    scale = 1.0 / math.sqrt(t * FFT_GC)
    af = (_mm(tab_ref[...], y2_ref[...]) * scale).astype(BF16)
    ya = _mm(af, w_ref[...])
    gate = _sigmoid(gm_ref[...].astype(F32) + bm_ref[...])
    y_ref[...] = (gate * ya).astype(BF16)


def _fft_call(z, layer, b, t, tables, w_fft_out, b_merge):
    m = z.shape[0]
    tr = min(t, 512)
    nr = t // tr
    tab, cc, sc = tables
    return pl.pallas_call(
        functools.partial(_fft_kernel, t=t),
        grid=(b, nr),
        in_specs=[
            pl.BlockSpec((t, FFT_W), lambda i, r: (i, COL_AF // FFT_W)),
            pl.BlockSpec((tr, D_MODEL), lambda i, r: (i * nr + r, COL_GM // D_MODEL)),
            pl.BlockSpec((tr, 2 * t), lambda i, r: (r, 0)),
            pl.BlockSpec((FFT_GC, FFT_GC), lambda i, r: (0, 0)),
            pl.BlockSpec((FFT_GC, FFT_GC), lambda i, r: (0, 0)),
            pl.BlockSpec((None, FFT_W, D_MODEL), lambda i, r: (layer, 0, 0)),
            pl.BlockSpec((None, None, 1, D_MODEL), lambda i, r: (layer, 0, 0, 0)),
        ],
        out_specs=pl.BlockSpec((tr, D_MODEL), lambda i, r: (i * nr + r, 0)),
        out_shape=jax.ShapeDtypeStruct((m, D_MODEL), BF16),
        scratch_shapes=[pltpu.VMEM((2 * t, FFT_W), BF16)],
        compiler_params=_params(("arbitrary", "arbitrary")),
        name="fft",
    )(z, z, tab, cc, sc, w_fft_out, b_merge)


def _sgu_kernel(u_ref, v_ref, gm_ref, g_ref, ws_ref, bias_ref, w_ref, bm_ref, y_ref, vm_ref):
    tm = u_ref.shape[0]
    v = _gelu(v_ref[...].astype(F32))
    vn = (_norm(v) * g_ref[...]).astype(BF16)
    for n in range(tm // SGU_CHUNK):
        rows = slice(n * SGU_CHUNK, (n + 1) * SGU_CHUNK)
        for g in range(SGU_GROUPS):
            cols = slice(g * SGU_GC, (g + 1) * SGU_GC)
            vm_ref[rows, cols] = _mm(ws_ref[g], vn[rows, cols]) + bias_ref[:, cols]
    u = _gelu(u_ref[...].astype(F32))
    yb = _mm((u * vm_ref[...]).astype(BF16), w_ref[...])
    gate = _sigmoid(gm_ref[...].astype(F32) + bm_ref[...])
    y_ref[...] = (gate * yb).astype(BF16)


def _sgu_call(z, layer, sgu_g, sgu_ws, sgu_bias, w_sgu_out, b_merge):
    m = z.shape[0]
    tm = 1024
    return pl.pallas_call(
        _sgu_kernel,
        grid=(m // tm,),
        in_specs=[
            pl.BlockSpec((tm, SGU_W), lambda i: (i, COL_U // SGU_W)),
            pl.BlockSpec((tm, SGU_W), lambda i: (i, COL_V // SGU_W)),
            pl.BlockSpec((tm, D_MODEL), lambda i: (i, COL_GM // D_MODEL + 1)),
            pl.BlockSpec((None, 1, SGU_W), lambda i: (layer, 0, 0)),
            pl.BlockSpec((None, SGU_GROUPS, SGU_CHUNK, SGU_CHUNK), lambda i: (layer, 0, 0, 0)),
            pl.BlockSpec((None, SGU_CHUNK, SGU_W), lambda i: (layer, 0, 0)),
            pl.BlockSpec((None, SGU_W, D_MODEL), lambda i: (layer, 0, 0)),
            pl.BlockSpec((None, None, 1, D_MODEL), lambda i: (layer, 1, 0, 0)),
        ],
        out_specs=pl.BlockSpec((tm, D_MODEL), lambda i: (i, 0)),
        out_shape=jax.ShapeDtypeStruct((m, D_MODEL), BF16),
        scratch_shapes=[pltpu.VMEM((tm, SGU_W), F32)],
        compiler_params=_params(("arbitrary",)),
        name="sgu",
    )(z, z, z, sgu_g, sgu_ws, sgu_bias, w_sgu_out, b_merge)


def _gla_kernel(*refs, t, has_s0, want_state):
    refs = list(refs)
    q_ref, k_ref, v_ref, r_ref, lr_ref, w2_ref, b2_ref = refs[:7]
    del refs[:7]
    s0_ref = refs.pop(0) if has_s0 else None
    ng_ref, wout_ref, gm_ref, bm_ref, y_ref = refs[:5]
    del refs[:5]
    sfin_ref = refs.pop(0) if want_state else None
    oall_ref, o_ref, lg_ref, qg_ref, kg_ref, ds_ref, ss_ref, gtot_ref = refs

    head = pl.program_id(1)
    ch = GLA_CHUNK
    n = t // ch
    gr = GLA_GROUP_ROWS
    cpg = gr // ch
    n_groups = t // gr
    unroll = min(n_groups, 8)
    nt = (((1,), (1,)), ((), ()))
    tn = (((0,), (0,)), ((), ()))

    lrb = lr_ref[...].astype(BF16)
    for d in range(2):
        zg = _mm(lrb, w2_ref[d]) + b2_ref[d]
        lg_ref[d] = (jnp.minimum(zg, 0.0) - jnp.log(1.0 + jnp.exp(-jnp.abs(zg)))) * (1.0 / GLA_GATE_NORM)

    def split3(x):
        hi = x.astype(BF16)
        rem = x - hi.astype(F32)
        mid = rem.astype(BF16)
        lo = (rem - mid.astype(F32)).astype(BF16)
        return hi, mid, lo

    row = lax.broadcasted_iota(jnp.int32, (gr, gr), 0)
    col = lax.broadcasted_iota(jnp.int32, (gr, gr), 1)
    same = (row // ch) == (col // ch)
    incl = (same & (col <= row), same & (col >= row))
    cum_mats = tuple(m.astype(BF16) for m in incl)
    brow = lax.broadcasted_iota(jnp.int32, (gr, cpg * GLA_DK), 0)
    bcol = lax.broadcasted_iota(jnp.int32, (gr, cpg * GLA_DK), 1)
    own_chunk = (brow // ch) == (bcol // GLA_DK)

    def spread(x):
        return jnp.where(own_chunk, jnp.concatenate([x] * cpg, axis=1), jnp.zeros((), x.dtype))

    if gtot_ref.shape[0] > n * 8:
        gtot_ref[...] = jnp.zeros_like(gtot_ref)

    for d in range(2):
        def prepare(g, carry, d=d):
            r0 = pl.multiple_of(g * gr, gr)
            lg3 = jnp.concatenate(split3(lg_ref[d, pl.ds(r0, gr), :]), axis=1)
            cums = _mm(cum_mats[d], lg3)
            gc = cums[:, 0:GLA_DK] + cums[:, GLA_DK:2 * GLA_DK] + cums[:, 2 * GLA_DK:3 * GLA_DK]
            g4 = gc.reshape(cpg, ch, GLA_DK)
            tot = g4[:, ch - 1:ch, :] if d == 0 else g4[:, 0:1, :]
            rc = (tot - g4).reshape(gr, GLA_DK)
            t0 = pl.multiple_of(g * (cpg * 8), cpg * 8)
            gtot_ref[pl.ds(t0, cpg * 8), :] = jnp.broadcast_to(tot, (cpg, 8, GLA_DK)).reshape(cpg * 8, GLA_DK)
            q = q_ref[pl.ds(r0, gr), :].astype(F32) * (GLA_DK ** -0.5)
            k = k_ref[pl.ds(r0, gr), :].astype(F32)
            qg_ref[pl.ds(r0, gr), :] = (q * jnp.exp(gc)).astype(BF16)
            kg_ref[pl.ds(r0, gr), :] = (k * jnp.exp(-gc)).astype(BF16)
            kd = (k * jnp.exp(rc)).astype(BF16)
            d0 = pl.multiple_of(g * (cpg * GLA_DK), cpg * GLA_DK)
            ds_ref[pl.ds(d0, cpg * GLA_DK), :] = lax.dot_general(
                spread(kd), v_ref[pl.ds(r0, gr), :], tn, preferred_element_type=F32)
            return carry

        lax.fori_loop(0, n_groups, prepare, 0, unroll=unroll)

        dec = jnp.exp(gtot_ref[...].T)
        s = s0_ref[d] if has_s0 else jnp.zeros((GLA_DK, GLA_DV), F32)
        for c in (range(n) if d == 0 else range(n - 1, -1, -1)):
            ss_ref[c * GLA_DK:(c + 1) * GLA_DK, :] = s.astype(BF16)
            s = s * dec[:, 8 * c:8 * c + 1] + ds_ref[c * GLA_DK:(c + 1) * GLA_DK, :]
        if want_state:
            sfin_ref[d] = s

        def attend(g, carry, d=d):
            r0 = pl.multiple_of(g * gr, gr)
            d0 = pl.multiple_of(g * (cpg * GLA_DK), cpg * GLA_DK)
            qg = qg_ref[pl.ds(r0, gr), :]
            a = lax.dot_general(qg, kg_ref[pl.ds(r0, gr), :], nt, preferred_element_type=F32)
            a = jnp.where(incl[d], a, 0.0).astype(BF16)
            o = _mm(a, v_ref[pl.ds(r0, gr), :]) + _mm(spread(qg), ss_ref[pl.ds(d0, cpg * GLA_DK), :])
            if d == 0:
                o_ref[pl.ds(r0, gr), :] = o
            else:
                o_ref[pl.ds(r0, gr), :] += o
            return carry

        lax.fori_loop(0, n_groups, attend, 0, unroll=unroll)

    o = o_ref[...]
    o = o * lax.rsqrt(jnp.mean(o * o, axis=-1, keepdims=True) + LN_EPS) * ng_ref[...]
    oall_ref[head] = (o * _silu(r_ref[...].astype(F32))).astype(BF16)

    @pl.when(head == pl.num_programs(1) - 1)
    def _():
        o_heads = jnp.concatenate([oall_ref[h] for h in range(GLA_H)], axis=1)
        gate = _sigmoid(gm_ref[...].astype(F32) + bm_ref[...])
        y_ref[...] = (gate * _mm(o_heads, wout_ref[...])).astype(BF16)


def _gla_call(z, lr, layer, b, t, state, w2h, b2h, gla_norm_g, w_gla_out, b_merge, want_state):
    m = z.shape[0]
    has_s0 = state is not None
    n_chunks = t // GLA_CHUNK
    in_specs = [
        pl.BlockSpec((t, GLA_DK), lambda i, h: (i, COL_Q // GLA_DK + h)),
        pl.BlockSpec((t, GLA_DK), lambda i, h: (i, COL_K // GLA_DK + h)),
        pl.BlockSpec((t, GLA_DV), lambda i, h: (i, COL_VV // GLA_DV + h)),
        pl.BlockSpec((t, GLA_DV), lambda i, h: (i, COL_R // GLA_DV + h)),
        pl.BlockSpec((t, LR_PAD), lambda i, h: (i, 0)),
        pl.BlockSpec((None, None, 2, LR_PAD, GLA_DK), lambda i, h: (layer, h, 0, 0, 0)),
        pl.BlockSpec((None, None, 2, 1, GLA_DK), lambda i, h: (layer, h, 0, 0, 0)),
    ]
    args = [z, z, z, z, lr, w2h, b2h]
    if has_s0:
        in_specs.append(pl.BlockSpec((None, None, 2, None, GLA_DK, GLA_DV), lambda i, h: (i, layer, 0, h, 0, 0)))
        args.append(state)
    in_specs += [
        pl.BlockSpec((None, 1, GLA_DV), lambda i, h: (layer, 0, 0)),
        pl.BlockSpec((None, GLA_H * GLA_DV, D_MODEL), lambda i, h: (layer, 0, 0)),
        pl.BlockSpec((t, D_MODEL), lambda i, h: (i, COL_GM // D_MODEL + 2)),
        pl.BlockSpec((None, None, 1, D_MODEL), lambda i, h: (layer, 2, 0, 0)),
    ]
    args += [gla_norm_g, w_gla_out, z, b_merge]
    out_specs = [pl.BlockSpec((t, D_MODEL), lambda i, h: (i, 0))]
    out_shape = [jax.ShapeDtypeStruct((m, D_MODEL), BF16)]
    if want_state:
        out_specs.append(pl.BlockSpec((None, 2, None, GLA_DK, GLA_DV), lambda i, h: (i, 0, h, 0, 0)))
        out_shape.append(jax.ShapeDtypeStruct((b, 2, GLA_H, GLA_DK, GLA_DV), F32))
    outs = pl.pallas_call(
        functools.partial(_gla_kernel, t=t, has_s0=has_s0, want_state=want_state),
        grid=(b, GLA_H),
        in_specs=in_specs,
        out_specs=out_specs,
        out_shape=out_shape,
        scratch_shapes=[
            pltpu.VMEM((GLA_H, t, GLA_DV), BF16),
            pltpu.VMEM((t, GLA_DV), F32),
            pltpu.VMEM((2, t, GLA_DK), F32),
            pltpu.VMEM((t, GLA_DK), BF16),
            pltpu.VMEM((t, GLA_DK), BF16),
            pltpu.VMEM((n_chunks * GLA_DK, GLA_DV), F32),
            pltpu.VMEM((n_chunks * GLA_DK, GLA_DV), BF16),
            pltpu.VMEM((max(n_chunks * 8, GLA_DK), GLA_DK), F32),
        ],
        compiler_params=_params(("arbitrary", "arbitrary")),
        name="gla",
    )(*args)
    return (outs[0], outs[1]) if want_state else (outs[0], None)


def _mix_kernel(ya_ref, yb_ref, yc_ref, x_ref, mod_ref, wo_ref, g_ref, b_ref, x1_ref, h2_ref, *, alpha):
    msum = (ya_ref[...].astype(F32) + yb_ref[...].astype(F32) + yc_ref[...].astype(F32)).astype(BF16)
    mix = _mm(msum, wo_ref[...])
    x1 = _norm(alpha * x_ref[...] + mod_ref[2:3, :] * mix) * g_ref[...] + b_ref[...]
    x1_ref[...] = x1
    h2_ref[...] = (_norm(x1) * (1.0 + mod_ref[4:5, :]) + mod_ref[3:4, :]).astype(BF16)


def _mix_call(ya, yb, yc, x, mod, layer, mod_row, w_o, ln1_g, ln1_b, alpha):
    m, d = x.shape
    tm = 512
    row = lambda i: (i, 0)
    vec = pl.BlockSpec((None, 1, d), lambda i: (layer, 0, 0))
    return pl.pallas_call(
        functools.partial(_mix_kernel, alpha=alpha),
        grid=(m // tm,),
        in_specs=[
            pl.BlockSpec((tm, d), row), pl.BlockSpec((tm, d), row), pl.BlockSpec((tm, d), row),
            pl.BlockSpec((tm, d), row),
            pl.BlockSpec((None, None, N_ADA, d), lambda i: (layer, mod_row((i * tm) // 1024), 0, 0)),
            pl.BlockSpec((None, d, d), lambda i: (layer, 0, 0)),
            vec, vec,
        ],
        out_specs=[pl.BlockSpec((tm, d), row), pl.BlockSpec((tm, d), row)],
        out_shape=[jax.ShapeDtypeStruct((m, d), F32), jax.ShapeDtypeStruct((m, d), BF16)],
        compiler_params=_params(("arbitrary",)),
        name="mix",
    )(ya, yb, yc, x, mod, w_o, ln1_g, ln1_b)


def _ffn_kernel(h2_ref, wa0_ref, wg0_ref, wa_ref, wg_ref, wdw_ref, bdw_ref, wd_ref, f_ref,
                a_even, g_even, a_odd, g_odd, wm_ref, *, seq, latent):
    k = pl.program_id(1)
    nk = pl.num_programs(1)
    n = pl.program_id(0) * nk + k

    rows, tc = g_even.shape
    cr = FFN_CHUNK_ROWS
    n_chunks = rows // cr
    period = GRID_W if latent else seq
    pad = GRID_W if latent else 0
    nl = cr // period

    def set_taps(wdw):
        pos = lax.broadcasted_iota(jnp.int32, (period, tc), 0)
        for i in (range(9) if latent else range(3, 6)):
            wi = jnp.broadcast_to(wdw[i:i + 1, :], (period, tc))
            if i % 3 == 0:
                wi = jnp.where(pos > 0, wi, 0.0)
            if i % 3 == 2:
                wi = jnp.where(pos < period - 1, wi, 0.0)
            wm_ref[i] = wi

    def up(r0, wa, wg, a_dst, g_dst):
        h2 = h2_ref[pl.ds(r0, cr), :]
        a_dst[pl.ds(r0 + pad, cr), :] = _mm(h2, wa[...])
        g_dst[pl.ds(r0, cr), :] = _mm(h2, wg[...])

    def conv_gate(r0, a_src, g_src):
        w = lambda i: wm_ref[i].reshape(1, period, tc)
        slab = a_src[pl.ds(r0, cr + 2 * pad), :].reshape((cr + 2 * pad) // period, period, tc)
        left = pltpu.roll(slab, 1, axis=1)
        right = pltpu.roll(slab, period - 1, axis=1)
        if latent:
            conv = sum(w(3 * dh) * left[dh:dh + nl] + w(3 * dh + 1) * slab[dh:dh + nl]
                       + w(3 * dh + 2) * right[dh:dh + nl] for dh in range(3))
        else:
            conv = w(3) * left + w(4) * slab + w(5) * right
        conv = conv.reshape(cr, tc) + bdw_ref[...]
        return (_gelu(conv) * g_src[pl.ds(r0, cr), :]).astype(BF16)

    def chunks(body, unroll):
        lax.fori_loop(0, n_chunks, lambda c, carry: (body(pl.multiple_of(c * cr, cr)), carry)[1], 0,
                      unroll=unroll)

    @pl.when(n == 0)
    def _():
        if pad:
            for a_ref in (a_even, a_odd):
                a_ref[0:pad, :] = jnp.zeros((pad, tc), F32)
                a_ref[pad + rows:pad + rows + pad, :] = jnp.zeros((pad, tc), F32)
        chunks(lambda r0: up(r0, wa0_ref, wg0_ref, a_even, g_even), 2)

    @pl.when(k == 0)
    def _():
        f_ref[...] = jnp.zeros_like(f_ref)

    def step(a_cur, g_cur, a_nxt, g_nxt):
        set_taps(wdw_ref)

        def body(r0):
            f_ref[pl.ds(r0, cr), :] += _mm(conv_gate(r0, a_cur, g_cur), wd_ref[...])
            up(r0, wa_ref, wg_ref, a_nxt, g_nxt)

        chunks(body, 8)

    @pl.when(n % 2 == 0)
    def _():
        step(a_even, g_even, a_odd, g_odd)

    @pl.when(n % 2 == 1)
    def _():
        step(a_odd, g_odd, a_even, g_even)


def _ffn_call(h2, layer, seq, latent, w_up, w_dw, b_dw, w_down):
    m, d = h2.shape
    rows = 2048
    tc = 256
    nk = D_FF // tc
    nb = m // rows
    if latent:
        assert rows == seq and FFN_CHUNK_ROWS % GRID_W == 0
    else:
        assert FFN_CHUNK_ROWS % seq == 0
    pad = GRID_W if latent else 0
    period = GRID_W if latent else seq
    up_block = lambda i, k: jnp.minimum((i * nk + k + 1) // nk, nb - 1)
    up_tile = lambda k: (k + 1) % nk
    return pl.pallas_call(
        functools.partial(_ffn_kernel, seq=seq, latent=latent),
        grid=(nb, nk),
        in_specs=[
            pl.BlockSpec((rows, d), lambda i, k: (up_block(i, k), 0)),
            pl.BlockSpec((None, d, tc), lambda i, k: (layer, 0, 0)),
            pl.BlockSpec((None, d, tc), lambda i, k: (layer, 0, nk)),
            pl.BlockSpec((None, d, tc), lambda i, k: (layer, 0, up_tile(k))),
            pl.BlockSpec((None, d, tc), lambda i, k: (layer, 0, nk + up_tile(k))),
            pl.BlockSpec((None, 9, tc), lambda i, k: (layer, 0, k)),
            pl.BlockSpec((None, 1, tc), lambda i, k: (layer, 0, k)),
            pl.BlockSpec((None, tc, d), lambda i, k: (layer, k, 0)),
        ],
        out_specs=pl.BlockSpec((rows, d), lambda i, k: (i, 0)),
        out_shape=jax.ShapeDtypeStruct((m, d), F32),
        scratch_shapes=[pltpu.VMEM((rows + 2 * pad, tc), F32), pltpu.VMEM((rows, tc), F32)] * 2
        + [pltpu.VMEM((9, period, tc), F32)],
        compiler_params=_params(("arbitrary", "arbitrary")),
        name="ffn",
    )(h2, w_up, w_up, w_up, w_up, w_dw, b_dw, w_down)


def _ln2_kernel(x_ref, f_ref, mod_ref, g_ref, b_ref, o_ref, *, alpha):
    o_ref[...] = _norm(alpha * x_ref[...] + mod_ref[5:6, :] * f_ref[...]) * g_ref[...] + b_ref[...]


def _ln2_call(x1, f, mod, layer, mod_row, ln2_g, ln2_b, alpha):
    m, d = x1.shape
    tm = 1024
    row = lambda i: (i, 0)
    vec = pl.BlockSpec((None, 1, d), lambda i: (layer, 0, 0))
    return pl.pallas_call(
        functools.partial(_ln2_kernel, alpha=alpha),
        grid=(m // tm,),
        in_specs=[
            pl.BlockSpec((tm, d), row), pl.BlockSpec((tm, d), row),
            pl.BlockSpec((None, None, N_ADA, d), lambda i: (layer, mod_row(i), 0, 0)),
            vec, vec,
        ],
        out_specs=pl.BlockSpec((tm, d), row),
        out_shape=jax.ShapeDtypeStruct((m, d), F32),
        compiler_params=_params(("arbitrary",)),
        name="ln2",
    )(x1, f, mod, ln2_g, ln2_b)


def kernel(x_prompt, x_sample, state_gla, c, c_ctx, w_in, b_merge, w_fft_out, sgu_g, sgu_ws, sgu_b, w_sgu_out, gla_w2, gla_b, gla_norm_g, w_gla_out, w_o, ln1_g, ln1_b, w_ada, b_ada, w_up, w_dw, b_dw, w_down, ln2_g, ln2_b):
    depth = w_in.shape[0]
    alpha = (2.0 * depth) ** 0.25
    bp, tp, d = x_prompt.shape
    bs, ts, _ = x_sample.shape

    n_cond = 16
    cond = jnp.zeros((n_cond, d), F32).at[:bs].set(c.astype(F32)).at[bs].set(c_ctx.astype(F32))
    mod = _ada_call(cond, w_ada, b_ada).reshape(depth, n_cond, N_ADA, d)

    cuts = np.cumsum([FFT_W, SGU_W, SGU_W, GLA_H * GLA_DK, GLA_H * GLA_DK, GLA_H * GLA_DV, GLA_H * GLA_DV, 2 * GLA_LR]).tolist()
    p_af, p_u, p_v, p_q, p_k, p_vv, p_r, p_lr, p_gm = jnp.split(w_in, cuts, axis=-1)
    w_main = jnp.concatenate([p_vv, p_r, p_gm, p_af, p_u, p_v, p_q, p_k], axis=-1).astype(BF16)
    w_lr = jnp.pad(p_lr, ((0, 0), (0, 0), (0, LR_PAD - 2 * GLA_LR))).astype(BF16)
    w2 = gla_w2.reshape(depth, 2, GLA_LR, GLA_H, GLA_DK).transpose(0, 3, 1, 2, 4)
    w2h = jnp.zeros((depth, GLA_H, 2, LR_PAD, GLA_DK), F32)
    w2h = w2h.at[:, :, 0, 0:GLA_LR].set(w2[:, :, 0]).at[:, :, 1, GLA_LR:2 * GLA_LR].set(w2[:, :, 1]).astype(BF16)
    b2h = gla_b.reshape(depth, 2, GLA_H, 1, GLA_DK).transpose(0, 2, 1, 3, 4).astype(F32)
    sgu_bias = jnp.repeat(sgu_b.transpose(0, 2, 1), SGU_GC, axis=-1).astype(F32)
    vec = lambda a: a.reshape(depth, 1, a.shape[-1]).astype(F32)
    bm = b_merge.reshape(depth, N_BRANCH, 1, d).astype(F32)
    w_fft_out_b, w_sgu_out_b, sgu_ws_b = w_fft_out.astype(BF16), w_sgu_out.astype(BF16), sgu_ws.astype(BF16)
    w_gla_out_b, w_o_b, w_up_b, w_down_b = w_gla_out.astype(BF16), w_o.astype(BF16), w_up.astype(BF16), w_down.astype(BF16)
    w_dw9 = w_dw.reshape(depth, 9, D_FF).astype(F32)
    sgu_g3, ng3, ln1g, ln1b, ln2g, ln2b, bdw = (vec(a) for a in (sgu_g, gla_norm_g, ln1_g, ln1_b, ln2_g, ln2_b, b_dw))

    streams = [
        dict(x=x_prompt.reshape(bp * tp, d), b=bp, t=tp, latent=False, state=None, mod_row=lambda i: bs),
        dict(x=x_sample.reshape(bs * ts, d), b=bs, t=ts, latent=True, state=state_gla.astype(F32),
             mod_row=lambda i: (i * 1024) // ts),
    ]
    new_states = []
    for s in streams:
        x, f = s["x"].astype(F32), None
        tables = tuple(jnp.asarray(a).astype(BF16) for a in _dft_tables(s["t"]))
        for l in range(depth):
            prev_ln = (ln2g, ln2b) if l else None
            x2, z, lr = _in_call(x, f, mod, l, s["mod_row"], prev_ln, w_main, w_lr, alpha)
            x = x2 if l else x
            ya = _fft_call(z, l, s["b"], s["t"], tables, w_fft_out_b, bm)
            yb = _sgu_call(z, l, sgu_g3, sgu_ws_b, sgu_bias, w_sgu_out_b, bm)
            yc, st = _gla_call(z, lr, l, s["b"], s["t"], s["state"], w2h, b2h, ng3, w_gla_out_b, bm,
                               want_state=not s["latent"])
            if st is not None:
                new_states.append(st)
            x, h2 = _mix_call(ya, yb, yc, x, mod, l, s["mod_row"], w_o_b, ln1g, ln1b, alpha)
            f = _ffn_call(h2, l, s["t"], s["latent"], w_up_b, w_dw9, bdw, w_down_b)
        s["y"] = _ln2_call(x, f, mod, depth - 1, s["mod_row"], ln2g, ln2b, alpha)

    y_prompt = streams[0]["y"].reshape(bp, tp, d).astype(x_prompt.dtype)
    y_sample = streams[1]["y"].reshape(bs, ts, d).astype(x_sample.dtype)
    new_state_gla = jnp.stack(new_states, axis=1).astype(x_prompt.dtype)
    return (y_prompt, y_sample, new_state_gla)
```

```python
import functools
import math

import numpy as np
import jax
import jax.numpy as jnp
from jax import lax
from jax.experimental import pallas as pl
from jax.experimental.pallas import tpu as pltpu

F32 = jnp.float32
BF16 = jnp.bfloat16

D_MODEL = 1024
GRID_W = 64
FFT_GROUPS = 4
FFT_GC = 128
FFT_W = FFT_GROUPS * FFT_GC
SGU_GROUPS = 4
SGU_GC = 128
SGU_W = SGU_GROUPS * SGU_GC
SGU_CHUNK = 128
GLA_H = 4
GLA_DK = 128
GLA_DV = 256
GLA_LR = 16
GLA_GATE_NORM = 16.0
GLA_CHUNK = 64
FFT_FLIP_ROWS = 256
FFT_OUT_ROWS = 512
GLA_BLOCK_ROWS = 2048
GLA_GROUP_ROWS = 256
D_FF = 2816
N_BRANCH = 3
N_ADA = 6
LN_EPS = 1e-5
LR_PAD = 128
IN_CHUNK_ROWS = 256
FFN_CHUNK_ROWS = 256

COL_VV = 0
COL_R = COL_VV + GLA_H * GLA_DV
COL_GM = COL_R + GLA_H * GLA_DV
COL_AF = COL_GM + N_BRANCH * D_MODEL
COL_U = COL_AF + FFT_W
COL_V = COL_U + SGU_W
COL_Q = COL_V + SGU_W
COL_K = COL_Q + GLA_H * GLA_DK
Z_W = COL_K + GLA_H * GLA_DK

V7X_VMEM_BYTES = 64 * 1024 * 1024
VMEM_LIMIT = V7X_VMEM_BYTES - 8 * 1024 * 1024


def _mm(a, b):
    return jnp.dot(a, b, preferred_element_type=F32)


def _sigmoid(x):
    return 0.5 * jnp.tanh(0.5 * x) + 0.5


def _silu(x):
    return x * _sigmoid(x)


def _gelu(x):
    c = math.sqrt(2.0 / math.pi)
    half = 0.5 * x
    return half + half * jnp.tanh(x * (c + (c * 0.044715) * (x * x)))


def _norm(x):
    mu = jnp.mean(x, axis=-1, keepdims=True)
    xc = x - mu
    var = jnp.mean(xc * xc, axis=-1, keepdims=True)
    return xc * lax.rsqrt(var + LN_EPS)


def _params(sem, limit=VMEM_LIMIT, flags=None):
    return pltpu.CompilerParams(dimension_semantics=sem, vmem_limit_bytes=limit, flags=flags)


def _ada_kernel(c_ref, w_ref, b_ref, o_ref):
    s = _silu(c_ref[...]).astype(BF16)
    o_ref[...] = _mm(s, w_ref[...].astype(BF16)) + b_ref[...]


def _ada_call(cond, w_ada, b_ada):
    depth, d, n = w_ada.shape
    rows = cond.shape[0]
    tn = 1536
    return pl.pallas_call(
        _ada_kernel,
        grid=(depth, n // tn),
        in_specs=[
            pl.BlockSpec((rows, d), lambda l, j: (0, 0)),
            pl.BlockSpec((None, d, tn), lambda l, j: (l, 0, j)),
            pl.BlockSpec((None, 1, tn), lambda l, j: (l, 0, j)),
        ],
        out_specs=pl.BlockSpec((None, rows, tn), lambda l, j: (l, 0, j)),
        out_shape=jax.ShapeDtypeStruct((depth, rows, n), F32),
        compiler_params=_params(("arbitrary", "arbitrary")),
        name="ada",
    )(cond, w_ada, b_ada.reshape(depth, 1, n))


def _in_kernel(*refs, alpha, has_prev):
    if has_prev:
        (x_ref, f_ref, pmod_ref, pg_ref, pb_ref, mod_ref, w_ref, wlr_ref,
         z_ref, lr_ref, x2_ref, h_ref) = refs
    else:
        x_ref, mod_ref, w_ref, wlr_ref, z_ref, lr_ref, h_ref = refs

    @pl.when(pl.program_id(1) == 0)
    def _():
        cr = IN_CHUNK_ROWS
        for c in range(x_ref.shape[0] // cr):
            rows = slice(c * cr, (c + 1) * cr)
            x = x_ref[rows, :]
            if has_prev:
                g2 = pmod_ref[5:6, :]
                x = _norm(alpha * x + g2 * f_ref[rows, :]) * pg_ref[...] + pb_ref[...]
                x2_ref[rows, :] = x
            h = (_norm(x) * (1.0 + mod_ref[1:2, :]) + mod_ref[0:1, :]).astype(BF16)
            h_ref[rows, :] = h
            lr_ref[rows, :] = _mm(h, wlr_ref[...])
            z_ref[rows, :] = _mm(h, w_ref[...]).astype(BF16)

    @pl.when(pl.program_id(1) != 0)
    def _():
        z_ref[...] = _mm(h_ref[...], w_ref[...]).astype(BF16)


def _in_call(x, f, mod, layer, mod_row, prev_ln, w_main, w_lr, alpha):
    m, d = x.shape
    tm = 1024
    tn = 1536
    has_prev = f is not None
    row = lambda i, j: (i, 0)
    in_specs = [pl.BlockSpec((tm, d), row)]
    args = [x]
    if has_prev:
        pg, pb = prev_ln
        in_specs += [
            pl.BlockSpec((tm, d), row),
            pl.BlockSpec((None, None, N_ADA, d), lambda i, j: (layer - 1, mod_row(i), 0, 0)),
            pl.BlockSpec((None, 1, d), lambda i, j: (layer - 1, 0, 0)),
            pl.BlockSpec((None, 1, d), lambda i, j: (layer - 1, 0, 0)),
        ]
        args += [f, mod, pg, pb]
    in_specs += [
        pl.BlockSpec((None, None, N_ADA, d), lambda i, j: (layer, mod_row(i), 0, 0)),
        pl.BlockSpec((None, d, tn), lambda i, j: (layer, 0, j)),
        pl.BlockSpec((None, d, LR_PAD), lambda i, j: (layer, 0, 0)),
    ]
    args += [mod, w_main, w_lr]
    out_specs = [pl.BlockSpec((tm, tn), lambda i, j: (i, j)), pl.BlockSpec((tm, LR_PAD), row)]
    out_shape = [jax.ShapeDtypeStruct((m, Z_W), BF16), jax.ShapeDtypeStruct((m, LR_PAD), F32)]
    if has_prev:
        out_specs.append(pl.BlockSpec((tm, d), row))
        out_shape.append(jax.ShapeDtypeStruct((m, d), F32))
    outs = pl.pallas_call(
        functools.partial(_in_kernel, alpha=alpha, has_prev=has_prev),
        grid=(m // tm, Z_W // tn),
        in_specs=in_specs,
        out_specs=out_specs,
        out_shape=out_shape,
        scratch_shapes=[pltpu.VMEM((tm, d), BF16)],
        compiler_params=_params(("arbitrary", "arbitrary")),
        name="in_proj",
    )(*args)
    if has_prev:
        z, lr, x2 = outs
        return x2, z, lr
    z, lr = outs
    return None, z, lr


@functools.lru_cache(maxsize=None)
def _dft_tables(t):
    def cs(rows, n):
        ang = (2.0 * np.pi / n) * ((np.arange(rows)[:, None] * np.arange(n)[None, :]) % n).astype(np.float64)
        return np.cos(ang).astype(np.float32), np.sin(ang).astype(np.float32)

    ct, st = cs(t // 2 + 8, t)
    cc, sc = cs(FFT_GC, FFT_GC)
    flip = np.eye(min(FFT_FLIP_ROWS, t // 2), dtype=np.float32)[::-1].copy()
    return ct, st, cc, sc, flip


def _fft_kernel(a_ref, gm_ref, ct_ref, st_ref, cc_ref, sc_ref, flip_ref, w_ref, bm_ref, y_ref,
                xc_ref, xs_ref, af_ref, *, t):
    half = t // 2
    fb = flip_ref.shape[0]
    for g in range(FFT_GROUPS):
        cols = slice(g * FFT_GC, (g + 1) * FFT_GC)
        xg = a_ref[:, cols]
        xc_ref[:, cols] = _mm(xg, cc_ref[...]).astype(BF16)
        xs_ref[:, cols] = _mm(xg, sc_ref[...]).astype(BF16)
    scale = 1.0 / math.sqrt(t * FFT_GC)
    p = _mm(ct_ref[...], xc_ref[...]) * scale
    q = _mm(st_ref[...], xs_ref[...]) * scale
    af_ref[0:half, :] = (p - q)[0:half].astype(BF16)
    mirrored = (p + q)[1:half + 1].astype(BF16)
    for i in range(half // fb):
        block = mirrored[half - (i + 1) * fb:half - i * fb]
        af_ref[half + i * fb:half + (i + 1) * fb, :] = _mm(flip_ref[...], block).astype(BF16)
    tr = min(t, FFT_OUT_ROWS)
    for r in range(t // tr):
        rows = slice(r * tr, (r + 1) * tr)
        gate = _sigmoid(gm_ref[rows, :].astype(F32) + bm_ref[...])
        y_ref[rows, :] = (gate * _mm(af_ref[rows, :], w_ref[...])).astype(BF16)


def _fft_call(z, layer, b, t, tables, w_fft_out, b_merge):
    m = z.shape[0]
    ct, st, cc, sc, flip = tables
    const = lambda a: pl.BlockSpec(a.shape, lambda i: (0, 0), pipeline_mode=pl.Buffered(1))
    return pl.pallas_call(
        functools.partial(_fft_kernel, t=t),
        grid=(b,),
        in_specs=[
            pl.BlockSpec((t, FFT_W), lambda i: (i, COL_AF // FFT_W)),
            pl.BlockSpec((t, D_MODEL), lambda i: (i, COL_GM // D_MODEL)),
            const(ct), const(st), const(cc), const(sc), const(flip),
            pl.BlockSpec((None, FFT_W, D_MODEL), lambda i: (layer, 0, 0)),
            pl.BlockSpec((None, None, 1, D_MODEL), lambda i: (layer, 0, 0, 0)),
        ],
        out_specs=pl.BlockSpec((t, D_MODEL), lambda i: (i, 0)),
        out_shape=jax.ShapeDtypeStruct((m, D_MODEL), BF16),
        scratch_shapes=[pltpu.VMEM((t, FFT_W), BF16)] * 3,
        compiler_params=_params(("arbitrary",)),
        name="fft",
    )(z, z, ct, st, cc, sc, flip, w_fft_out, b_merge)


def _sgu_kernel(u_ref, v_ref, gm_ref, g_ref, ws_ref, bias_ref, w_ref, bm_ref, y_ref, vm_ref):
    tm = u_ref.shape[0]
    v = _gelu(v_ref[...].astype(F32))
    vn = (_norm(v) * g_ref[...]).astype(BF16)
    for n in range(tm // SGU_CHUNK):
        rows = slice(n * SGU_CHUNK, (n + 1) * SGU_CHUNK)
        for g in range(SGU_GROUPS):
            cols = slice(g * SGU_GC, (g + 1) * SGU_GC)
            vm_ref[rows, cols] = _mm(ws_ref[g], vn[rows, cols]) + bias_ref[:, cols]
    u = _gelu(u_ref[...].astype(F32))
    yb = _mm((u * vm_ref[...]).astype(BF16), w_ref[...])
    gate = _sigmoid(gm_ref[...].astype(F32) + bm_ref[...])
    y_ref[...] = (gate * yb).astype(BF16)


def _sgu_call(z, layer, sgu_g, sgu_ws, sgu_bias, w_sgu_out, b_merge):
    m = z.shape[0]
    tm = 1024
    return pl.pallas_call(
        _sgu_kernel,
        grid=(m // tm,),
        in_specs=[
            pl.BlockSpec((tm, SGU_W), lambda i: (i, COL_U // SGU_W)),
            pl.BlockSpec((tm, SGU_W), lambda i: (i, COL_V // SGU_W)),
            pl.BlockSpec((tm, D_MODEL), lambda i: (i, COL_GM // D_MODEL + 1)),
            pl.BlockSpec((None, 1, SGU_W), lambda i: (layer, 0, 0)),
            pl.BlockSpec((None, SGU_GROUPS, SGU_CHUNK, SGU_CHUNK), lambda i: (layer, 0, 0, 0)),
            pl.BlockSpec((None, SGU_CHUNK, SGU_W), lambda i: (layer, 0, 0)),
            pl.BlockSpec((None, SGU_W, D_MODEL), lambda i: (layer, 0, 0)),
            pl.BlockSpec((None, None, 1, D_MODEL), lambda i: (layer, 1, 0, 0)),
        ],
        out_specs=pl.BlockSpec((tm, D_MODEL), lambda i: (i, 0)),
        out_shape=jax.ShapeDtypeStruct((m, D_MODEL), BF16),
        scratch_shapes=[pltpu.VMEM((tm, SGU_W), F32)],
        compiler_params=_params(("arbitrary",)),
        name="sgu",
    )(z, z, z, sgu_g, sgu_ws, sgu_bias, w_sgu_out, b_merge)


def _gla_kernel(*refs, t, seq, has_s0, want_state):
    refs = list(refs)
    q_ref, k_ref, v_ref, r_ref, lr_ref, w2_ref, b2_ref = refs[:7]
    del refs[:7]
    s0_ref = refs.pop(0) if has_s0 else None
    ng_ref, wout_ref, gm_ref, bm_ref, y_ref = refs[:5]
    del refs[:5]
    sfin_ref = refs.pop(0) if want_state else None
    oall_ref, o_ref, lg_ref, qg_ref, kg_ref, ds_ref, ss_ref, gtot_ref = refs

    head = pl.program_id(1)
    ch = GLA_CHUNK
    n = t // ch
    gr = GLA_GROUP_ROWS
    cpg = gr // ch
    n_groups = t // gr
    unroll = min(n_groups, 8)
    nt = (((1,), (1,)), ((), ()))
    tn = (((0,), (0,)), ((), ()))

    lrb = lr_ref[...].astype(BF16)
    for d in range(2):
        zg = _mm(lrb, w2_ref[d]) + b2_ref[d]
        lg_ref[d] = (jnp.minimum(zg, 0.0) - jnp.log(1.0 + jnp.exp(-jnp.abs(zg)))) * (1.0 / GLA_GATE_NORM)

    def split3(x):
        hi = x.astype(BF16)
        rem = x - hi.astype(F32)
        mid = rem.astype(BF16)
        lo = (rem - mid.astype(F32)).astype(BF16)
        return hi, mid, lo

    row = lax.broadcasted_iota(jnp.int32, (gr, gr), 0)
    col = lax.broadcasted_iota(jnp.int32, (gr, gr), 1)
    same = (row // ch) == (col // ch)
    incl = (same & (col <= row), same & (col >= row))
    cum_mats = tuple(m.astype(BF16) for m in incl)
    brow = lax.broadcasted_iota(jnp.int32, (gr, cpg * GLA_DK), 0)
    bcol = lax.broadcasted_iota(jnp.int32, (gr, cpg * GLA_DK), 1)
    own_chunk = (brow // ch) == (bcol // GLA_DK)

    def spread(x):
        return jnp.where(own_chunk, jnp.concatenate([x] * cpg, axis=1), jnp.zeros((), x.dtype))

    if gtot_ref.shape[0] > n * 8:
        gtot_ref[...] = jnp.zeros_like(gtot_ref)

    for d in range(2):
        def prepare(g, carry, d=d):
            r0 = pl.multiple_of(g * gr, gr)
            lg3 = jnp.concatenate(split3(lg_ref[d, pl.ds(r0, gr), :]), axis=1)
            cums = _mm(cum_mats[d], lg3)
            gc = cums[:, 0:GLA_DK] + cums[:, GLA_DK:2 * GLA_DK] + cums[:, 2 * GLA_DK:3 * GLA_DK]
            g4 = gc.reshape(cpg, ch, GLA_DK)
            tot = g4[:, ch - 1:ch, :] if d == 0 else g4[:, 0:1, :]
            rc = (tot - g4).reshape(gr, GLA_DK)
            t0 = pl.multiple_of(g * (cpg * 8), cpg * 8)
            gtot_ref[pl.ds(t0, cpg * 8), :] = jnp.broadcast_to(tot, (cpg, 8, GLA_DK)).reshape(cpg * 8, GLA_DK)
            q = q_ref[pl.ds(r0, gr), :].astype(F32) * (GLA_DK ** -0.5)
            k = k_ref[pl.ds(r0, gr), :].astype(F32)
            qg_ref[pl.ds(r0, gr), :] = (q * jnp.exp(gc)).astype(BF16)
            kg_ref[pl.ds(r0, gr), :] = (k * jnp.exp(-gc)).astype(BF16)
            kd = (k * jnp.exp(rc)).astype(BF16)
            d0 = pl.multiple_of(g * (cpg * GLA_DK), cpg * GLA_DK)
            ds_ref[pl.ds(d0, cpg * GLA_DK), :] = lax.dot_general(
                spread(kd), v_ref[pl.ds(r0, gr), :], tn, preferred_element_type=F32)
            return carry

        lax.fori_loop(0, n_groups, prepare, 0, unroll=unroll)

        dec = jnp.exp(gtot_ref[...].T)
        cps = seq // ch
        first_c, last_c = (0, cps - 1) if d == 0 else (cps - 1, 0)
        for c in (range(n) if d == 0 else range(n - 1, -1, -1)):
            if c % cps == first_c:
                s = s0_ref[d] if has_s0 else jnp.zeros((GLA_DK, GLA_DV), F32)
            ss_ref[c * GLA_DK:(c + 1) * GLA_DK, :] = s.astype(BF16)
            s = s * dec[:, 8 * c:8 * c + 1] + ds_ref[c * GLA_DK:(c + 1) * GLA_DK, :]
            if want_state and c % cps == last_c:
                sfin_ref[c // cps, d] = s

        def attend(g, carry, d=d):
            r0 = pl.multiple_of(g * gr, gr)
            d0 = pl.multiple_of(g * (cpg * GLA_DK), cpg * GLA_DK)
            qg = qg_ref[pl.ds(r0, gr), :]
            a = lax.dot_general(qg, kg_ref[pl.ds(r0, gr), :], nt, preferred_element_type=F32)
            a = jnp.where(incl[d], a, 0.0).astype(BF16)
            o = _mm(a, v_ref[pl.ds(r0, gr), :]) + _mm(spread(qg), ss_ref[pl.ds(d0, cpg * GLA_DK), :])
            if d == 0:
                o_ref[pl.ds(r0, gr), :] = o
            else:
                o_ref[pl.ds(r0, gr), :] += o
            return carry

        lax.fori_loop(0, n_groups, attend, 0, unroll=unroll)

    o = o_ref[...]
    o = o * lax.rsqrt(jnp.mean(o * o, axis=-1, keepdims=True) + LN_EPS) * ng_ref[...]
    oall_ref[head] = (o * _silu(r_ref[...].astype(F32))).astype(BF16)

    @pl.when(head == pl.num_programs(1) - 1)
    def _():
        o_heads = jnp.concatenate([oall_ref[h] for h in range(GLA_H)], axis=1)
        gate = _sigmoid(gm_ref[...].astype(F32) + bm_ref[...])
        y_ref[...] = (gate * _mm(o_heads, wout_ref[...])).astype(BF16)


def _gla_call(z, lr, layer, b, seq, state, w2h, b2h, gla_norm_g, w_gla_out, b_merge, want_state):
    m = z.shape[0]
    has_s0 = state is not None
    t = seq if has_s0 else max(seq, min(GLA_BLOCK_ROWS, m))
    n_seq = t // seq
    assert t % seq == 0 and m % t == 0 and seq % GLA_GROUP_ROWS == 0
    n_chunks = t // GLA_CHUNK
    in_specs = [
        pl.BlockSpec((t, GLA_DK), lambda i, h: (i, COL_Q // GLA_DK + h)),
        pl.BlockSpec((t, GLA_DK), lambda i, h: (i, COL_K // GLA_DK + h)),
        pl.BlockSpec((t, GLA_DV), lambda i, h: (i, COL_VV // GLA_DV + h)),
        pl.BlockSpec((t, GLA_DV), lambda i, h: (i, COL_R // GLA_DV + h)),
        pl.BlockSpec((t, LR_PAD), lambda i, h: (i, 0)),
        pl.BlockSpec((None, None, 2, LR_PAD, GLA_DK), lambda i, h: (layer, h, 0, 0, 0)),
        pl.BlockSpec((None, None, 2, 1, GLA_DK), lambda i, h: (layer, h, 0, 0, 0)),
    ]
    args = [z, z, z, z, lr, w2h, b2h]
    if has_s0:
        in_specs.append(pl.BlockSpec((None, None, 2, None, GLA_DK, GLA_DV), lambda i, h: (i, layer, 0, h, 0, 0)))
        args.append(state)
    in_specs += [
        pl.BlockSpec((None, 1, GLA_DV), lambda i, h: (layer, 0, 0)),
        pl.BlockSpec((None, GLA_H * GLA_DV, D_MODEL), lambda i, h: (layer, 0, 0)),
        pl.BlockSpec((t, D_MODEL), lambda i, h: (i, COL_GM // D_MODEL + 2)),
        pl.BlockSpec((None, None, 1, D_MODEL), lambda i, h: (layer, 2, 0, 0)),
    ]
    args += [gla_norm_g, w_gla_out, z, b_merge]
    out_specs = [pl.BlockSpec((t, D_MODEL), lambda i, h: (i, 0))]
    out_shape = [jax.ShapeDtypeStruct((m, D_MODEL), BF16)]
    if want_state:
        out_specs.append(pl.BlockSpec((n_seq, 2, None, GLA_DK, GLA_DV), lambda i, h: (i, 0, h, 0, 0)))
        out_shape.append(jax.ShapeDtypeStruct((b, 2, GLA_H, GLA_DK, GLA_DV), F32))
    outs = pl.pallas_call(
        functools.partial(_gla_kernel, t=t, seq=seq, has_s0=has_s0, want_state=want_state),
        grid=(m // t, GLA_H),
        in_specs=in_specs,
        out_specs=out_specs,
        out_shape=out_shape,
        scratch_shapes=[
            pltpu.VMEM((GLA_H, t, GLA_DV), BF16),
            pltpu.VMEM((t, GLA_DV), F32),
            pltpu.VMEM((2, t, GLA_DK), F32),
            pltpu.VMEM((t, GLA_DK), BF16),
            pltpu.VMEM((t, GLA_DK), BF16),
            pltpu.VMEM((n_chunks * GLA_DK, GLA_DV), F32),
            pltpu.VMEM((n_chunks * GLA_DK, GLA_DV), BF16),
            pltpu.VMEM((max(n_chunks * 8, GLA_DK), GLA_DK), F32),
        ],
        compiler_params=_params(("arbitrary", "arbitrary")),
        name="gla",
    )(*args)
    return (outs[0], outs[1]) if want_state else (outs[0], None)


def _mix_kernel(ya_ref, yb_ref, yc_ref, x_ref, mod_ref, wo_ref, g_ref, b_ref, x1_ref, h2_ref, *, alpha):
    msum = (ya_ref[...].astype(F32) + yb_ref[...].astype(F32) + yc_ref[...].astype(F32)).astype(BF16)
    mix = _mm(msum, wo_ref[...])
    x1 = _norm(alpha * x_ref[...] + mod_ref[2:3, :] * mix) * g_ref[...] + b_ref[...]
    x1_ref[...] = x1
    h2_ref[...] = (_norm(x1) * (1.0 + mod_ref[4:5, :]) + mod_ref[3:4, :]).astype(BF16)


def _mix_call(ya, yb, yc, x, mod, layer, mod_row, w_o, ln1_g, ln1_b, alpha):
    m, d = x.shape
    tm = 512
    row = lambda i: (i, 0)
    vec = pl.BlockSpec((None, 1, d), lambda i: (layer, 0, 0))
    return pl.pallas_call(
        functools.partial(_mix_kernel, alpha=alpha),
        grid=(m // tm,),
        in_specs=[
            pl.BlockSpec((tm, d), row), pl.BlockSpec((tm, d), row), pl.BlockSpec((tm, d), row),
            pl.BlockSpec((tm, d), row),
            pl.BlockSpec((None, None, N_ADA, d), lambda i: (layer, mod_row((i * tm) // 1024), 0, 0)),
            pl.BlockSpec((None, d, d), lambda i: (layer, 0, 0)),
            vec, vec,
        ],
        out_specs=[pl.BlockSpec((tm, d), row), pl.BlockSpec((tm, d), row)],
        out_shape=[jax.ShapeDtypeStruct((m, d), F32), jax.ShapeDtypeStruct((m, d), BF16)],
        compiler_params=_params(("arbitrary",)),
        name="mix",
    )(ya, yb, yc, x, mod, w_o, ln1_g, ln1_b)


def _ffn_kernel(h2_ref, wa0_ref, wg0_ref, wa_ref, wg_ref, wdw_ref, bdw_ref, wd_ref, f_ref,
                a_even, g_even, a_odd, g_odd, wm_ref, *, seq, latent):
    k = pl.program_id(1)
    nk = pl.num_programs(1)
    n = pl.program_id(0) * nk + k

    rows, tc = g_even.shape
    cr = FFN_CHUNK_ROWS
    n_chunks = rows // cr
    period = GRID_W if latent else seq
    pad = GRID_W if latent else 0
    nl = cr // period

    def set_taps(wdw):
        pos = lax.broadcasted_iota(jnp.int32, (period, tc), 0)
        for i in (range(9) if latent else range(3, 6)):
            wi = jnp.broadcast_to(wdw[i:i + 1, :], (period, tc))
            if i % 3 == 0:
                wi = jnp.where(pos > 0, wi, 0.0)
            if i % 3 == 2:
                wi = jnp.where(pos < period - 1, wi, 0.0)
            wm_ref[i] = wi

    def up(r0, wa, wg, a_dst, g_dst):
        h2 = h2_ref[pl.ds(r0, cr), :]
        a_dst[pl.ds(r0 + pad, cr), :] = _mm(h2, wa[...])
        g_dst[pl.ds(r0, cr), :] = _mm(h2, wg[...])

    def conv_gate(r0, a_src, g_src):
        w = lambda i: wm_ref[i].reshape(1, period, tc)
        slab = a_src[pl.ds(r0, cr + 2 * pad), :].reshape((cr + 2 * pad) // period, period, tc)
        left = pltpu.roll(slab, 1, axis=1)
        right = pltpu.roll(slab, period - 1, axis=1)
        if latent:
            conv = sum(w(3 * dh) * left[dh:dh + nl] + w(3 * dh + 1) * slab[dh:dh + nl]
                       + w(3 * dh + 2) * right[dh:dh + nl] for dh in range(3))
        else:
            conv = w(3) * left + w(4) * slab + w(5) * right
        conv = conv.reshape(cr, tc) + bdw_ref[...]
        return (_gelu(conv) * g_src[pl.ds(r0, cr), :]).astype(BF16)

    def chunks(body, unroll):
        lax.fori_loop(0, n_chunks, lambda c, carry: (body(pl.multiple_of(c * cr, cr)), carry)[1], 0,
                      unroll=unroll)

    @pl.when(n == 0)
    def _():
        if pad:
            for a_ref in (a_even, a_odd):
                a_ref[0:pad, :] = jnp.zeros((pad, tc), F32)
                a_ref[pad + rows:pad + rows + pad, :] = jnp.zeros((pad, tc), F32)
        chunks(lambda r0: up(r0, wa0_ref, wg0_ref, a_even, g_even), 2)

    @pl.when(k == 0)
    def _():
        f_ref[...] = jnp.zeros_like(f_ref)

    def step(a_cur, g_cur, a_nxt, g_nxt):
        set_taps(wdw_ref)

        def body(r0):
            f_ref[pl.ds(r0, cr), :] += _mm(conv_gate(r0, a_cur, g_cur), wd_ref[...])
            up(r0, wa_ref, wg_ref, a_nxt, g_nxt)

        chunks(body, 8)

    @pl.when(n % 2 == 0)
    def _():
        step(a_even, g_even, a_odd, g_odd)

    @pl.when(n % 2 == 1)
    def _():
        step(a_odd, g_odd, a_even, g_even)


def _ffn_call(h2, layer, seq, latent, w_up, w_dw, b_dw, w_down):
    m, d = h2.shape
    rows = 2048
    tc = 256
    nk = D_FF // tc
    nb = m // rows
    if latent:
        assert rows == seq and FFN_CHUNK_ROWS % GRID_W == 0
    else:
        assert FFN_CHUNK_ROWS % seq == 0
    pad = GRID_W if latent else 0
    period = GRID_W if latent else seq
    up_block = lambda i, k: jnp.minimum((i * nk + k + 1) // nk, nb - 1)
    up_tile = lambda k: (k + 1) % nk
    return pl.pallas_call(
        functools.partial(_ffn_kernel, seq=seq, latent=latent),
        grid=(nb, nk),
        in_specs=[
            pl.BlockSpec((rows, d), lambda i, k: (up_block(i, k), 0)),
            pl.BlockSpec((None, d, tc), lambda i, k: (layer, 0, 0)),
            pl.BlockSpec((None, d, tc), lambda i, k: (layer, 0, nk)),
            pl.BlockSpec((None, d, tc), lambda i, k: (layer, 0, up_tile(k))),
            pl.BlockSpec((None, d, tc), lambda i, k: (layer, 0, nk + up_tile(k))),
            pl.BlockSpec((None, 9, tc), lambda i, k: (layer, 0, k)),
            pl.BlockSpec((None, 1, tc), lambda i, k: (layer, 0, k)),
            pl.BlockSpec((None, tc, d), lambda i, k: (layer, k, 0)),
        ],
        out_specs=pl.BlockSpec((rows, d), lambda i, k: (i, 0)),
        out_shape=jax.ShapeDtypeStruct((m, d), F32),
        scratch_shapes=[pltpu.VMEM((rows + 2 * pad, tc), F32), pltpu.VMEM((rows, tc), F32)] * 2
        + [pltpu.VMEM((9, period, tc), F32)],
        compiler_params=_params(("arbitrary", "arbitrary")),
        name="ffn",
    )(h2, w_up, w_up, w_up, w_up, w_dw, b_dw, w_down)


def _ln2_kernel(x_ref, f_ref, mod_ref, g_ref, b_ref, o_ref, *, alpha):
    o_ref[...] = _norm(alpha * x_ref[...] + mod_ref[5:6, :] * f_ref[...]) * g_ref[...] + b_ref[...]


def _ln2_call(x1, f, mod, layer, mod_row, ln2_g, ln2_b, alpha):
    m, d = x1.shape
    tm = 1024
    row = lambda i: (i, 0)
    vec = pl.BlockSpec((None, 1, d), lambda i: (layer, 0, 0))
    return pl.pallas_call(
        functools.partial(_ln2_kernel, alpha=alpha),
        grid=(m // tm,),
        in_specs=[
            pl.BlockSpec((tm, d), row), pl.BlockSpec((tm, d), row),
            pl.BlockSpec((None, None, N_ADA, d), lambda i: (layer, mod_row(i), 0, 0)),
            vec, vec,
        ],
        out_specs=pl.BlockSpec((tm, d), row),
        out_shape=jax.ShapeDtypeStruct((m, d), F32),
        compiler_params=_params(("arbitrary",)),
        name="ln2",
    )(x1, f, mod, ln2_g, ln2_b)


def kernel(x_prompt, x_sample, state_gla, c, c_ctx, w_in, b_merge, w_fft_out, sgu_g, sgu_ws, sgu_b, w_sgu_out, gla_w2, gla_b, gla_norm_g, w_gla_out, w_o, ln1_g, ln1_b, w_ada, b_ada, w_up, w_dw, b_dw, w_down, ln2_g, ln2_b):
    depth = w_in.shape[0]
    alpha = (2.0 * depth) ** 0.25
    bp, tp, d = x_prompt.shape
    bs, ts, _ = x_sample.shape

    n_cond = 16
    cond = jnp.zeros((n_cond, d), F32).at[:bs].set(c.astype(F32)).at[bs].set(c_ctx.astype(F32))
    mod = _ada_call(cond, w_ada, b_ada).reshape(depth, n_cond, N_ADA, d)

    cuts = np.cumsum([FFT_W, SGU_W, SGU_W, GLA_H * GLA_DK, GLA_H * GLA_DK, GLA_H * GLA_DV, GLA_H * GLA_DV, 2 * GLA_LR]).tolist()
    p_af, p_u, p_v, p_q, p_k, p_vv, p_r, p_lr, p_gm = jnp.split(w_in, cuts, axis=-1)
    w_main = jnp.concatenate([p_vv, p_r, p_gm, p_af, p_u, p_v, p_q, p_k], axis=-1).astype(BF16)
    w_lr = jnp.pad(p_lr, ((0, 0), (0, 0), (0, LR_PAD - 2 * GLA_LR))).astype(BF16)
    w2 = gla_w2.reshape(depth, 2, GLA_LR, GLA_H, GLA_DK).transpose(0, 3, 1, 2, 4)
    w2h = jnp.zeros((depth, GLA_H, 2, LR_PAD, GLA_DK), F32)
    w2h = w2h.at[:, :, 0, 0:GLA_LR].set(w2[:, :, 0]).at[:, :, 1, GLA_LR:2 * GLA_LR].set(w2[:, :, 1]).astype(BF16)
    b2h = gla_b.reshape(depth, 2, GLA_H, 1, GLA_DK).transpose(0, 2, 1, 3, 4).astype(F32)
    sgu_bias = jnp.repeat(sgu_b.transpose(0, 2, 1), SGU_GC, axis=-1).astype(F32)
    vec = lambda a: a.reshape(depth, 1, a.shape[-1]).astype(F32)
    bm = b_merge.reshape(depth, N_BRANCH, 1, d).astype(F32)
    w_fft_out_b, w_sgu_out_b, sgu_ws_b = w_fft_out.astype(BF16), w_sgu_out.astype(BF16), sgu_ws.astype(BF16)
    w_gla_out_b, w_o_b, w_up_b, w_down_b = w_gla_out.astype(BF16), w_o.astype(BF16), w_up.astype(BF16), w_down.astype(BF16)
    w_dw9 = w_dw.reshape(depth, 9, D_FF).astype(F32)
    sgu_g3, ng3, ln1g, ln1b, ln2g, ln2b, bdw = (vec(a) for a in (sgu_g, gla_norm_g, ln1_g, ln1_b, ln2_g, ln2_b, b_dw))

    streams = [
        dict(x=x_prompt.reshape(bp * tp, d), b=bp, t=tp, latent=False, state=None, mod_row=lambda i: bs),
        dict(x=x_sample.reshape(bs * ts, d), b=bs, t=ts, latent=True, state=state_gla.astype(F32),
             mod_row=lambda i: (i * 1024) // ts),
    ]
    new_states = []
    for s in streams:
        x, f = s["x"].astype(F32), None
        tables = tuple(jnp.asarray(a).astype(BF16) for a in _dft_tables(s["t"]))
        for l in range(depth):
            prev_ln = (ln2g, ln2b) if l else None
            x2, z, lr = _in_call(x, f, mod, l, s["mod_row"], prev_ln, w_main, w_lr, alpha)
            x = x2 if l else x
            ya = _fft_call(z, l, s["b"], s["t"], tables, w_fft_out_b, bm)
            yb = _sgu_call(z, l, sgu_g3, sgu_ws_b, sgu_bias, w_sgu_out_b, bm)
            yc, st = _gla_call(z, lr, l, s["b"], s["t"], s["state"], w2h, b2h, ng3, w_gla_out_b, bm,
                               want_state=not s["latent"])
            if st is not None:
                new_states.append(st)
            x, h2 = _mix_call(ya, yb, yc, x, mod, l, s["mod_row"], w_o_b, ln1g, ln1b, alpha)
            f = _ffn_call(h2, l, s["t"], s["latent"], w_up_b, w_dw9, bdw, w_down_b)
        s["y"] = _ln2_call(x, f, mod, depth - 1, s["mod_row"], ln2g, ln2b, alpha)

    y_prompt = streams[0]["y"].reshape(bp, tp, d).astype(x_prompt.dtype)
    y_sample = streams[1]["y"].reshape(bs, ts, d).astype(x_sample.dtype)
    new_state_gla = jnp.stack(new_states, axis=1).astype(x_prompt.dtype)
    return (y_prompt, y_sample, new_state_gla)
```

```python
import functools
import math

import numpy as np
import jax
import jax.numpy as jnp
from jax import lax
from jax.experimental import pallas as pl
from jax.experimental.pallas import tpu as pltpu

F32 = jnp.float32
BF16 = jnp.bfloat16

D_MODEL = 1024
GRID_W = 64
FFT_GROUPS = 4
FFT_GC = 128
FFT_W = FFT_GROUPS * FFT_GC
SGU_GROUPS = 4
SGU_GC = 128
SGU_W = SGU_GROUPS * SGU_GC
SGU_CHUNK = 128
GLA_H = 4
GLA_DK = 128
GLA_DV = 256
GLA_LR = 16
GLA_GATE_NORM = 16.0
GLA_CHUNK = 64
FFT_FLIP_ROWS = 256
FFT_OUT_ROWS = 512
GLA_BLOCK_ROWS = 2048
GLA_GROUP_ROWS = 256
D_FF = 2816
N_BRANCH = 3
N_ADA = 6
LN_EPS = 1e-5
LR_PAD = 128
IN_CHUNK_ROWS = 256
FFN_CHUNK_ROWS = 256

COL_GM = 0
COL_AF = COL_GM + N_BRANCH * D_MODEL
COL_U = COL_AF + FFT_W
COL_V = COL_U + SGU_W
COL_Q = COL_V + SGU_W
COL_K = COL_Q + GLA_H * GLA_DK
COL_VV = COL_K + GLA_H * GLA_DK
COL_R = COL_VV + GLA_H * GLA_DV
Z_W = COL_R + GLA_H * GLA_DV
IN_TN = 1536

V7X_VMEM_BYTES = 64 * 1024 * 1024
VMEM_LIMIT = V7X_VMEM_BYTES - 8 * 1024 * 1024


def _mm(a, b):
    return jnp.dot(a, b, preferred_element_type=F32)


def _sigmoid(x):
    return 0.5 * jnp.tanh(0.5 * x) + 0.5


def _silu(x):
    return x * _sigmoid(x)


def _gelu(x):
    c = math.sqrt(2.0 / math.pi)
    half = 0.5 * x
    return half + half * jnp.tanh(x * (c + (c * 0.044715) * (x * x)))


def _norm(x):
    mu = jnp.mean(x, axis=-1, keepdims=True)
    xc = x - mu
    var = jnp.mean(xc * xc, axis=-1, keepdims=True)
    return xc * lax.rsqrt(var + LN_EPS)


def _params(sem, limit=VMEM_LIMIT, flags=None):
    return pltpu.CompilerParams(dimension_semantics=sem, vmem_limit_bytes=limit, flags=flags)


def _ada_kernel(c_ref, w_ref, b_ref, o_ref):
    s = _silu(c_ref[...]).astype(BF16)
    o_ref[...] = _mm(s, w_ref[...].astype(BF16)) + b_ref[...]


def _ada_call(cond, w_ada, b_ada):
    depth, d, n = w_ada.shape
    rows = cond.shape[0]
    tn = 1536
    return pl.pallas_call(
        _ada_kernel,
        grid=(depth, n // tn),
        in_specs=[
            pl.BlockSpec((rows, d), lambda l, j: (0, 0)),
            pl.BlockSpec((None, d, tn), lambda l, j: (l, 0, j)),
            pl.BlockSpec((None, 1, tn), lambda l, j: (l, 0, j)),
        ],
        out_specs=pl.BlockSpec((None, rows, tn), lambda l, j: (l, 0, j)),
        out_shape=jax.ShapeDtypeStruct((depth, rows, n), F32),
        compiler_params=_params(("arbitrary", "arbitrary")),
        name="ada",
    )(cond, w_ada, b_ada.reshape(depth, 1, n))


def _in_kernel(*refs, alpha, has_prev):
    if has_prev:
        (x_ref, f_ref, pmod_ref, pg_ref, pb_ref, mod_ref, wgm_ref, w_ref, wlr_ref,
         z_ref, lr_ref, x2_ref, h_ref) = refs
    else:
        x_ref, mod_ref, wgm_ref, w_ref, wlr_ref, z_ref, lr_ref, h_ref = refs
    j = pl.program_id(1)
    n_gate_tiles = N_BRANCH * D_MODEL // IN_TN

    @pl.when(j == 0)
    def _():
        cr = IN_CHUNK_ROWS
        for c in range(x_ref.shape[0] // cr):
            rows = slice(c * cr, (c + 1) * cr)
            x = x_ref[rows, :]
            if has_prev:
                g2 = pmod_ref[5:6, :]
                x = _norm(alpha * x + g2 * f_ref[rows, :]) * pg_ref[...] + pb_ref[...]
                x2_ref[rows, :] = x
            h = (_norm(x) * (1.0 + mod_ref[1:2, :]) + mod_ref[0:1, :]).astype(BF16)
            h_ref[rows, :] = h
            lr_ref[rows, :] = _mm(h, wlr_ref[...])
            z_ref[rows, :] = _mm(h, wgm_ref[...]).astype(BF16)

    @pl.when((j != 0) & (j < n_gate_tiles))
    def _():
        z_ref[...] = _mm(h_ref[...], wgm_ref[...]).astype(BF16)

    @pl.when(j >= n_gate_tiles)
    def _():
        z_ref[...] = _mm(h_ref[...], w_ref[...]).astype(BF16)


def _in_call(x, f, mod, layer, mod_row, prev_ln, w_gm, w_main, w_lr, alpha):
    m, d = x.shape
    tm = 1024
    tn = IN_TN
    n_gate_tiles = N_BRANCH * D_MODEL // tn
    has_prev = f is not None
    row = lambda i, j: (i, 0)
    in_specs = [pl.BlockSpec((tm, d), row)]
    args = [x]
    if has_prev:
        pg, pb = prev_ln
        in_specs += [
            pl.BlockSpec((tm, d), row),
            pl.BlockSpec((None, None, N_ADA, d), lambda i, j: (layer - 1, mod_row(i), 0, 0)),
            pl.BlockSpec((None, 1, d), lambda i, j: (layer - 1, 0, 0)),
            pl.BlockSpec((None, 1, d), lambda i, j: (layer - 1, 0, 0)),
        ]
        args += [f, mod, pg, pb]
    in_specs += [
        pl.BlockSpec((None, None, N_ADA, d), lambda i, j: (layer, mod_row(i), 0, 0)),
        pl.BlockSpec((None, d, tn), lambda i, j: (layer, 0, jnp.minimum(j, n_gate_tiles - 1))),
        pl.BlockSpec((None, d, tn), lambda i, j: (layer, 0, jnp.maximum(j - n_gate_tiles, 0))),
        pl.BlockSpec((None, d, LR_PAD), lambda i, j: (layer, 0, 0)),
    ]
    args += [mod, w_gm, w_main, w_lr]
    out_specs = [pl.BlockSpec((tm, tn), lambda i, j: (i, j)), pl.BlockSpec((tm, LR_PAD), row)]
    out_shape = [jax.ShapeDtypeStruct((m, Z_W), BF16), jax.ShapeDtypeStruct((m, LR_PAD), F32)]
    if has_prev:
        out_specs.append(pl.BlockSpec((tm, d), row))
        out_shape.append(jax.ShapeDtypeStruct((m, d), F32))
    outs = pl.pallas_call(
        functools.partial(_in_kernel, alpha=alpha, has_prev=has_prev),
        grid=(m // tm, Z_W // tn),
        in_specs=in_specs,
        out_specs=out_specs,
        out_shape=out_shape,
        scratch_shapes=[pltpu.VMEM((tm, d), BF16)],
        compiler_params=_params(("arbitrary", "arbitrary")),
        name="in_proj",
    )(*args)
    if has_prev:
        z, lr, x2 = outs
        return x2, z, lr
    z, lr = outs
    return None, z, lr


@functools.lru_cache(maxsize=None)
def _dft_tables(t):
    def cs(rows, n):
        ang = (2.0 * np.pi / n) * ((np.arange(rows)[:, None] * np.arange(n)[None, :]) % n).astype(np.float64)
        return np.cos(ang).astype(np.float32), np.sin(ang).astype(np.float32)

    ct, st = cs(t // 2 + 8, t)
    cc, sc = cs(FFT_GC, FFT_GC)
    flip = np.eye(min(FFT_FLIP_ROWS, t // 2), dtype=np.float32)[::-1].copy()
    return ct, st, cc, sc, flip


def _fft_kernel(a_ref, gm_ref, ct_ref, st_ref, cc_ref, sc_ref, flip_ref, w_ref, bm_ref, y_ref,
                xc_ref, xs_ref, af_ref, *, t):
    half = t // 2
    fb = flip_ref.shape[0]
    for g in range(FFT_GROUPS):
        cols = slice(g * FFT_GC, (g + 1) * FFT_GC)
        xg = a_ref[:, cols]
        xc_ref[:, cols] = _mm(xg, cc_ref[...]).astype(BF16)
        xs_ref[:, cols] = _mm(xg, sc_ref[...]).astype(BF16)
    scale = 1.0 / math.sqrt(t * FFT_GC)
    p = _mm(ct_ref[...], xc_ref[...]) * scale
    q = _mm(st_ref[...], xs_ref[...]) * scale
    af_ref[0:half, :] = (p - q)[0:half].astype(BF16)
    mirrored = (p + q)[1:half + 1].astype(BF16)
    for i in range(half // fb):
        block = mirrored[half - (i + 1) * fb:half - i * fb]
        af_ref[half + i * fb:half + (i + 1) * fb, :] = _mm(flip_ref[...], block).astype(BF16)
    tr = min(t, FFT_OUT_ROWS)
    for r in range(t // tr):
        rows = slice(r * tr, (r + 1) * tr)
        gate = _sigmoid(gm_ref[rows, :].astype(F32) + bm_ref[...])
        y_ref[rows, :] = (gate * _mm(af_ref[rows, :], w_ref[...])).astype(BF16)


def _fft_call(z, layer, b, t, tables, w_fft_out, b_merge):
    m = z.shape[0]
    ct, st, cc, sc, flip = tables
    const = lambda a: pl.BlockSpec(a.shape, lambda i: (0, 0), pipeline_mode=pl.Buffered(1))
    return pl.pallas_call(
        functools.partial(_fft_kernel, t=t),
        grid=(b,),
        in_specs=[
            pl.BlockSpec((t, FFT_W), lambda i: (i, COL_AF // FFT_W)),
            pl.BlockSpec((t, D_MODEL), lambda i: (i, COL_GM // D_MODEL)),
            const(ct), const(st), const(cc), const(sc), const(flip),
            pl.BlockSpec((None, FFT_W, D_MODEL), lambda i: (layer, 0, 0)),
            pl.BlockSpec((None, None, 1, D_MODEL), lambda i: (layer, 0, 0, 0)),
        ],
        out_specs=pl.BlockSpec((t, D_MODEL), lambda i: (i, 0)),
        out_shape=jax.ShapeDtypeStruct((m, D_MODEL), BF16),
        scratch_shapes=[pltpu.VMEM((t, FFT_W), BF16)] * 3,
        compiler_params=_params(("arbitrary",)),
        name="fft",
    )(z, z, ct, st, cc, sc, flip, w_fft_out, b_merge)


def _sgu_kernel(u_ref, v_ref, gm_ref, g_ref, ws_ref, bias_ref, w_ref, bm_ref, y_ref, vm_ref):
    tm = u_ref.shape[0]
    v = _gelu(v_ref[...].astype(F32))
    vn = (_norm(v) * g_ref[...]).astype(BF16)
    for n in range(tm // SGU_CHUNK):
        rows = slice(n * SGU_CHUNK, (n + 1) * SGU_CHUNK)
        for g in range(SGU_GROUPS):
            cols = slice(g * SGU_GC, (g + 1) * SGU_GC)
            vm_ref[rows, cols] = _mm(ws_ref[g], vn[rows, cols]) + bias_ref[:, cols]
    u = _gelu(u_ref[...].astype(F32))
    yb = _mm((u * vm_ref[...]).astype(BF16), w_ref[...])
    gate = _sigmoid(gm_ref[...].astype(F32) + bm_ref[...])
    y_ref[...] = (gate * yb).astype(BF16)


def _sgu_call(z, layer, sgu_g, sgu_ws, sgu_bias, w_sgu_out, b_merge):
    m = z.shape[0]
    tm = 1024
    return pl.pallas_call(
        _sgu_kernel,
        grid=(m // tm,),
        in_specs=[
            pl.BlockSpec((tm, SGU_W), lambda i: (i, COL_U // SGU_W)),
            pl.BlockSpec((tm, SGU_W), lambda i: (i, COL_V // SGU_W)),
            pl.BlockSpec((tm, D_MODEL), lambda i: (i, COL_GM // D_MODEL + 1)),
            pl.BlockSpec((None, 1, SGU_W), lambda i: (layer, 0, 0)),
            pl.BlockSpec((None, SGU_GROUPS, SGU_CHUNK, SGU_CHUNK), lambda i: (layer, 0, 0, 0)),
            pl.BlockSpec((None, SGU_CHUNK, SGU_W), lambda i: (layer, 0, 0)),
            pl.BlockSpec((None, SGU_W, D_MODEL), lambda i: (layer, 0, 0)),
            pl.BlockSpec((None, None, 1, D_MODEL), lambda i: (layer, 1, 0, 0)),
        ],
        out_specs=pl.BlockSpec((tm, D_MODEL), lambda i: (i, 0)),
        out_shape=jax.ShapeDtypeStruct((m, D_MODEL), BF16),
        scratch_shapes=[pltpu.VMEM((tm, SGU_W), F32)],
        compiler_params=_params(("arbitrary",)),
        name="sgu",
    )(z, z, z, sgu_g, sgu_ws, sgu_bias, w_sgu_out, b_merge)


def _gla_kernel(*refs, t, seq, has_s0, want_state):
    refs = list(refs)
    q_ref, k_ref, v_ref, r_ref, lr_ref, w2_ref, b2_ref = refs[:7]
    del refs[:7]
    s0_ref = refs.pop(0) if has_s0 else None
    ng_ref, wout_ref, gm_ref, bm_ref, y_ref = refs[:5]
    del refs[:5]
    sfin_ref = refs.pop(0) if want_state else None
    oall_ref, o_ref, lg_ref, qg_ref, kg_ref, ds_ref, ss_ref, gtot_ref = refs

    head = pl.program_id(1)
    ch = GLA_CHUNK
    n = t // ch
    gr = GLA_GROUP_ROWS
    cpg = gr // ch
    n_groups = t // gr
    unroll = min(n_groups, 8)
    nt = (((1,), (1,)), ((), ()))
    tn = (((0,), (0,)), ((), ()))

    lrb = lr_ref[...].astype(BF16)
    for d in range(2):
        zg = _mm(lrb, w2_ref[d]) + b2_ref[d]
        lg_ref[d] = (jnp.minimum(zg, 0.0) - jnp.log(1.0 + jnp.exp(-jnp.abs(zg)))) * (1.0 / GLA_GATE_NORM)

    def split3(x):
        hi = x.astype(BF16)
        rem = x - hi.astype(F32)
        mid = rem.astype(BF16)
        lo = (rem - mid.astype(F32)).astype(BF16)
        return hi, mid, lo

    row = lax.broadcasted_iota(jnp.int32, (gr, gr), 0)
    col = lax.broadcasted_iota(jnp.int32, (gr, gr), 1)
    same = (row // ch) == (col // ch)
    incl = (same & (col <= row), same & (col >= row))
    cum_mats = tuple(m.astype(BF16) for m in incl)
    brow = lax.broadcasted_iota(jnp.int32, (gr, cpg * GLA_DK), 0)
    bcol = lax.broadcasted_iota(jnp.int32, (gr, cpg * GLA_DK), 1)
    own_chunk = (brow // ch) == (bcol // GLA_DK)

    def spread(x):
        return jnp.where(own_chunk, jnp.concatenate([x] * cpg, axis=1), jnp.zeros((), x.dtype))

    if gtot_ref.shape[0] > n * 8:
        gtot_ref[...] = jnp.zeros_like(gtot_ref)

    for d in range(2):
        def prepare(g, carry, d=d):
            r0 = pl.multiple_of(g * gr, gr)
            lg3 = jnp.concatenate(split3(lg_ref[d, pl.ds(r0, gr), :]), axis=1)
            cums = _mm(cum_mats[d], lg3)
            gc = cums[:, 0:GLA_DK] + cums[:, GLA_DK:2 * GLA_DK] + cums[:, 2 * GLA_DK:3 * GLA_DK]
            g4 = gc.reshape(cpg, ch, GLA_DK)
            tot = g4[:, ch - 1:ch, :] if d == 0 else g4[:, 0:1, :]
            rc = (tot - g4).reshape(gr, GLA_DK)
            t0 = pl.multiple_of(g * (cpg * 8), cpg * 8)
            gtot_ref[pl.ds(t0, cpg * 8), :] = jnp.broadcast_to(tot, (cpg, 8, GLA_DK)).reshape(cpg * 8, GLA_DK)
            q = q_ref[pl.ds(r0, gr), :].astype(F32) * (GLA_DK ** -0.5)
            k = k_ref[pl.ds(r0, gr), :].astype(F32)
            qg_ref[pl.ds(r0, gr), :] = (q * jnp.exp(gc)).astype(BF16)
            kg_ref[pl.ds(r0, gr), :] = (k * jnp.exp(-gc)).astype(BF16)
            kd = (k * jnp.exp(rc)).astype(BF16)
            d0 = pl.multiple_of(g * (cpg * GLA_DK), cpg * GLA_DK)
            ds_ref[pl.ds(d0, cpg * GLA_DK), :] = lax.dot_general(
                spread(kd), v_ref[pl.ds(r0, gr), :], tn, preferred_element_type=F32)
            return carry

        lax.fori_loop(0, n_groups, prepare, 0, unroll=unroll)

        dec = jnp.exp(gtot_ref[...].T)
        cps = seq // ch
        first_c, last_c = (0, cps - 1) if d == 0 else (cps - 1, 0)
        for c in (range(n) if d == 0 else range(n - 1, -1, -1)):
            if c % cps == first_c:
                s = s0_ref[d] if has_s0 else jnp.zeros((GLA_DK, GLA_DV), F32)
            ss_ref[c * GLA_DK:(c + 1) * GLA_DK, :] = s.astype(BF16)
            s = s * dec[:, 8 * c:8 * c + 1] + ds_ref[c * GLA_DK:(c + 1) * GLA_DK, :]
            if want_state and c % cps == last_c:
                sfin_ref[c // cps, d] = s

        def attend(g, carry, d=d):
            r0 = pl.multiple_of(g * gr, gr)
            d0 = pl.multiple_of(g * (cpg * GLA_DK), cpg * GLA_DK)
            qg = qg_ref[pl.ds(r0, gr), :]
            a = lax.dot_general(qg, kg_ref[pl.ds(r0, gr), :], nt, preferred_element_type=F32)
            a = jnp.where(incl[d], a, 0.0).astype(BF16)
            o = _mm(a, v_ref[pl.ds(r0, gr), :]) + _mm(spread(qg), ss_ref[pl.ds(d0, cpg * GLA_DK), :])
            if d == 0:
                o_ref[pl.ds(r0, gr), :] = o
            else:
                o_ref[pl.ds(r0, gr), :] += o
            return carry

        lax.fori_loop(0, n_groups, attend, 0, unroll=unroll)

    o = o_ref[...]
    o = o * lax.rsqrt(jnp.mean(o * o, axis=-1, keepdims=True) + LN_EPS) * ng_ref[...]
    oall_ref[head] = (o * _silu(r_ref[...].astype(F32))).astype(BF16)

    @pl.when(head == pl.num_programs(1) - 1)
    def _():
        o_heads = jnp.concatenate([oall_ref[h] for h in range(GLA_H)], axis=1)
        gate = _sigmoid(gm_ref[...].astype(F32) + bm_ref[...])
        y_ref[...] = (gate * _mm(o_heads, wout_ref[...])).astype(BF16)


def _gla_call(z, lr, layer, b, seq, state, w2h, b2h, gla_norm_g, w_gla_out, b_merge, want_state):
    m = z.shape[0]
    has_s0 = state is not None
    t = seq if has_s0 else max(seq, min(GLA_BLOCK_ROWS, m))
    n_seq = t // seq
    assert t % seq == 0 and m % t == 0 and seq % GLA_GROUP_ROWS == 0
    n_chunks = t // GLA_CHUNK
    in_specs = [
        pl.BlockSpec((t, GLA_DK), lambda i, h: (i, COL_Q // GLA_DK + h)),
        pl.BlockSpec((t, GLA_DK), lambda i, h: (i, COL_K // GLA_DK + h)),
        pl.BlockSpec((t, GLA_DV), lambda i, h: (i, COL_VV // GLA_DV + h)),
        pl.BlockSpec((t, GLA_DV), lambda i, h: (i, COL_R // GLA_DV + h)),
        pl.BlockSpec((t, LR_PAD), lambda i, h: (i, 0)),
        pl.BlockSpec((None, None, 2, LR_PAD, GLA_DK), lambda i, h: (layer, h, 0, 0, 0)),
        pl.BlockSpec((None, None, 2, 1, GLA_DK), lambda i, h: (layer, h, 0, 0, 0)),
    ]
    args = [z, z, z, z, lr, w2h, b2h]
    if has_s0:
        in_specs.append(pl.BlockSpec((None, None, 2, None, GLA_DK, GLA_DV), lambda i, h: (i, layer, 0, h, 0, 0)))
        args.append(state)
    in_specs += [
        pl.BlockSpec((None, 1, GLA_DV), lambda i, h: (layer, 0, 0)),
        pl.BlockSpec((None, GLA_H * GLA_DV, D_MODEL), lambda i, h: (layer, 0, 0)),
        pl.BlockSpec((t, D_MODEL), lambda i, h: (i, COL_GM // D_MODEL + 2)),
        pl.BlockSpec((None, None, 1, D_MODEL), lambda i, h: (layer, 2, 0, 0)),
    ]
    args += [gla_norm_g, w_gla_out, z, b_merge]
    out_specs = [pl.BlockSpec((t, D_MODEL), lambda i, h: (i, 0))]
    out_shape = [jax.ShapeDtypeStruct((m, D_MODEL), BF16)]
    if want_state:
        out_specs.append(pl.BlockSpec((n_seq, 2, None, GLA_DK, GLA_DV), lambda i, h: (i, 0, h, 0, 0)))
        out_shape.append(jax.ShapeDtypeStruct((b, 2, GLA_H, GLA_DK, GLA_DV), F32))
    outs = pl.pallas_call(
        functools.partial(_gla_kernel, t=t, seq=seq, has_s0=has_s0, want_state=want_state),
        grid=(m // t, GLA_H),
        in_specs=in_specs,
        out_specs=out_specs,
        out_shape=out_shape,
        scratch_shapes=[
            pltpu.VMEM((GLA_H, t, GLA_DV), BF16),
            pltpu.VMEM((t, GLA_DV), F32),
            pltpu.VMEM((2, t, GLA_DK), F32),
            pltpu.VMEM((t, GLA_DK), BF16),
            pltpu.VMEM((t, GLA_DK), BF16),
            pltpu.VMEM((n_chunks * GLA_DK, GLA_DV), F32),
            pltpu.VMEM((n_chunks * GLA_DK, GLA_DV), BF16),
            pltpu.VMEM((max(n_chunks * 8, GLA_DK), GLA_DK), F32),
        ],
        compiler_params=_params(("arbitrary", "arbitrary")),
        name="gla",
    )(*args)
    return (outs[0], outs[1]) if want_state else (outs[0], None)


def _mix_kernel(ya_ref, yb_ref, yc_ref, x_ref, mod_ref, wo_ref, g_ref, b_ref, x1_ref, h2_ref, *, alpha):
    msum = (ya_ref[...].astype(F32) + yb_ref[...].astype(F32) + yc_ref[...].astype(F32)).astype(BF16)
    mix = _mm(msum, wo_ref[...])
    x1 = _norm(alpha * x_ref[...] + mod_ref[2:3, :] * mix) * g_ref[...] + b_ref[...]
    x1_ref[...] = x1
    h2_ref[...] = (_norm(x1) * (1.0 + mod_ref[4:5, :]) + mod_ref[3:4, :]).astype(BF16)


def _mix_call(ya, yb, yc, x, mod, layer, mod_row, w_o, ln1_g, ln1_b, alpha):
    m, d = x.shape
    tm = 512
    row = lambda i: (i, 0)
    vec = pl.BlockSpec((None, 1, d), lambda i: (layer, 0, 0))
    return pl.pallas_call(
        functools.partial(_mix_kernel, alpha=alpha),
        grid=(m // tm,),
        in_specs=[
            pl.BlockSpec((tm, d), row), pl.BlockSpec((tm, d), row), pl.BlockSpec((tm, d), row),
            pl.BlockSpec((tm, d), row),
            pl.BlockSpec((None, None, N_ADA, d), lambda i: (layer, mod_row((i * tm) // 1024), 0, 0)),
            pl.BlockSpec((None, d, d), lambda i: (layer, 0, 0)),
            vec, vec,
        ],
        out_specs=[pl.BlockSpec((tm, d), row), pl.BlockSpec((tm, d), row)],
        out_shape=[jax.ShapeDtypeStruct((m, d), F32), jax.ShapeDtypeStruct((m, d), BF16)],
        compiler_params=_params(("arbitrary",)),
        name="mix",
    )(ya, yb, yc, x, mod, w_o, ln1_g, ln1_b)


def _ffn_kernel(h2_ref, wa0_ref, wg0_ref, wa_ref, wg_ref, wdw_ref, bdw_ref, wd_ref, f_ref,
                a_even, g_even, a_odd, g_odd, wm_ref, *, seq, latent):
    k = pl.program_id(1)
    nk = pl.num_programs(1)
    n = pl.program_id(0) * nk + k

    rows, tc = g_even.shape
    cr = FFN_CHUNK_ROWS
    n_chunks = rows // cr
    period = GRID_W if latent else seq
    pad = GRID_W if latent else 0
    nl = cr // period

    def set_taps(wdw):
        pos = lax.broadcasted_iota(jnp.int32, (period, tc), 0)
        for i in (range(9) if latent else range(3, 6)):
            wi = jnp.broadcast_to(wdw[i:i + 1, :], (period, tc))
            if i % 3 == 0:
                wi = jnp.where(pos > 0, wi, 0.0)
            if i % 3 == 2:
                wi = jnp.where(pos < period - 1, wi, 0.0)
            wm_ref[i] = wi

    def up(r0, wa, wg, a_dst, g_dst):
        h2 = h2_ref[pl.ds(r0, cr), :]
        a_dst[pl.ds(r0 + pad, cr), :] = _mm(h2, wa[...])
        g_dst[pl.ds(r0, cr), :] = _mm(h2, wg[...])

    def conv_gate(r0, a_src, g_src):
        w = lambda i: wm_ref[i].reshape(1, period, tc)
        slab = a_src[pl.ds(r0, cr + 2 * pad), :].reshape((cr + 2 * pad) // period, period, tc)
        left = pltpu.roll(slab, 1, axis=1)
        right = pltpu.roll(slab, period - 1, axis=1)
        if latent:
            conv = sum(w(3 * dh) * left[dh:dh + nl] + w(3 * dh + 1) * slab[dh:dh + nl]
                       + w(3 * dh + 2) * right[dh:dh + nl] for dh in range(3))
        else:
            conv = w(3) * left + w(4) * slab + w(5) * right
        conv = conv.reshape(cr, tc) + bdw_ref[...]
        return (_gelu(conv) * g_src[pl.ds(r0, cr), :]).astype(BF16)

    def chunks(body, unroll):
        lax.fori_loop(0, n_chunks, lambda c, carry: (body(pl.multiple_of(c * cr, cr)), carry)[1], 0,
                      unroll=unroll)

    @pl.when(n == 0)
    def _():
        if pad:
            for a_ref in (a_even, a_odd):
                a_ref[0:pad, :] = jnp.zeros((pad, tc), F32)
                a_ref[pad + rows:pad + rows + pad, :] = jnp.zeros((pad, tc), F32)
        chunks(lambda r0: up(r0, wa0_ref, wg0_ref, a_even, g_even), 2)

    @pl.when(k == 0)
    def _():
        f_ref[...] = jnp.zeros_like(f_ref)

    def step(a_cur, g_cur, a_nxt, g_nxt):
        set_taps(wdw_ref)

        def body(r0):
            f_ref[pl.ds(r0, cr), :] += _mm(conv_gate(r0, a_cur, g_cur), wd_ref[...])
            up(r0, wa_ref, wg_ref, a_nxt, g_nxt)

        chunks(body, 8)

    @pl.when(n % 2 == 0)
    def _():
        step(a_even, g_even, a_odd, g_odd)

    @pl.when(n % 2 == 1)
    def _():
        step(a_odd, g_odd, a_even, g_even)


def _ffn_call(h2, layer, seq, latent, w_up, w_dw, b_dw, w_down):
    m, d = h2.shape
    rows = 2048
    tc = 256
    nk = D_FF // tc
    nb = m // rows
    if latent:
        assert rows == seq and FFN_CHUNK_ROWS % GRID_W == 0
    else:
        assert FFN_CHUNK_ROWS % seq == 0
    pad = GRID_W if latent else 0
    period = GRID_W if latent else seq
    up_block = lambda i, k: jnp.minimum((i * nk + k + 1) // nk, nb - 1)
    up_tile = lambda k: (k + 1) % nk
    return pl.pallas_call(
        functools.partial(_ffn_kernel, seq=seq, latent=latent),
        grid=(nb, nk),
        in_specs=[
            pl.BlockSpec((rows, d), lambda i, k: (up_block(i, k), 0)),
            pl.BlockSpec((None, d, tc), lambda i, k: (layer, 0, 0)),
            pl.BlockSpec((None, d, tc), lambda i, k: (layer, 0, nk)),
            pl.BlockSpec((None, d, tc), lambda i, k: (layer, 0, up_tile(k))),
            pl.BlockSpec((None, d, tc), lambda i, k: (layer, 0, nk + up_tile(k))),
            pl.BlockSpec((None, 9, tc), lambda i, k: (layer, 0, k)),
            pl.BlockSpec((None, 1, tc), lambda i, k: (layer, 0, k)),
            pl.BlockSpec((None, tc, d), lambda i, k: (layer, k, 0)),
        ],
        out_specs=pl.BlockSpec((rows, d), lambda i, k: (i, 0)),
        out_shape=jax.ShapeDtypeStruct((m, d), F32),
        scratch_shapes=[pltpu.VMEM((rows + 2 * pad, tc), F32), pltpu.VMEM((rows, tc), F32)] * 2
        + [pltpu.VMEM((9, period, tc), F32)],
        compiler_params=_params(("arbitrary", "arbitrary")),
        name="ffn",
    )(h2, w_up, w_up, w_up, w_up, w_dw, b_dw, w_down)


def _ln2_kernel(x_ref, f_ref, mod_ref, g_ref, b_ref, o_ref, *, alpha):
    o_ref[...] = _norm(alpha * x_ref[...] + mod_ref[5:6, :] * f_ref[...]) * g_ref[...] + b_ref[...]


def _ln2_call(x1, f, mod, layer, mod_row, ln2_g, ln2_b, alpha):
    m, d = x1.shape
    tm = 1024
    row = lambda i: (i, 0)
    vec = pl.BlockSpec((None, 1, d), lambda i: (layer, 0, 0))
    return pl.pallas_call(
        functools.partial(_ln2_kernel, alpha=alpha),
        grid=(m // tm,),
        in_specs=[
            pl.BlockSpec((tm, d), row), pl.BlockSpec((tm, d), row),
            pl.BlockSpec((None, None, N_ADA, d), lambda i: (layer, mod_row(i), 0, 0)),
            vec, vec,
        ],
        out_specs=pl.BlockSpec((tm, d), row),
        out_shape=jax.ShapeDtypeStruct((m, d), F32),
        compiler_params=_params(("arbitrary",)),
        name="ln2",
    )(x1, f, mod, ln2_g, ln2_b)


def kernel(x_prompt, x_sample, state_gla, c, c_ctx, w_in, b_merge, w_fft_out, sgu_g, sgu_ws, sgu_b, w_sgu_out, gla_w2, gla_b, gla_norm_g, w_gla_out, w_o, ln1_g, ln1_b, w_ada, b_ada, w_up, w_dw, b_dw, w_down, ln2_g, ln2_b):
    depth = w_in.shape[0]
    alpha = (2.0 * depth) ** 0.25
    bp, tp, d = x_prompt.shape
    bs, ts, _ = x_sample.shape

    n_cond = 16
    cond = jnp.zeros((n_cond, d), F32).at[:bs].set(c.astype(F32)).at[bs].set(c_ctx.astype(F32))
    mod = _ada_call(cond, w_ada, b_ada).reshape(depth, n_cond, N_ADA, d)

    n_main = Z_W - N_BRANCH * D_MODEL
    w_main = w_in[:, :, :n_main].astype(BF16)
    w_gm = w_in[:, :, n_main + 2 * GLA_LR:].astype(BF16)
    w_lr = jnp.pad(w_in[:, :, n_main:n_main + 2 * GLA_LR], ((0, 0), (0, 0), (0, LR_PAD - 2 * GLA_LR))).astype(BF16)
    w2 = gla_w2.reshape(depth, 2, GLA_LR, GLA_H, GLA_DK).transpose(0, 3, 1, 2, 4)
    w2h = jnp.zeros((depth, GLA_H, 2, LR_PAD, GLA_DK), F32)
    w2h = w2h.at[:, :, 0, 0:GLA_LR].set(w2[:, :, 0]).at[:, :, 1, GLA_LR:2 * GLA_LR].set(w2[:, :, 1]).astype(BF16)
    b2h = gla_b.reshape(depth, 2, GLA_H, 1, GLA_DK).transpose(0, 2, 1, 3, 4).astype(F32)
    sgu_bias = jnp.repeat(sgu_b.transpose(0, 2, 1), SGU_GC, axis=-1).astype(F32)
    vec = lambda a: a.reshape(depth, 1, a.shape[-1]).astype(F32)
    bm = b_merge.reshape(depth, N_BRANCH, 1, d).astype(F32)
    w_fft_out_b, w_sgu_out_b, sgu_ws_b = w_fft_out.astype(BF16), w_sgu_out.astype(BF16), sgu_ws.astype(BF16)
    w_gla_out_b, w_o_b, w_up_b, w_down_b = w_gla_out.astype(BF16), w_o.astype(BF16), w_up.astype(BF16), w_down.astype(BF16)
    w_dw9 = w_dw.reshape(depth, 9, D_FF).astype(F32)
    sgu_g3, ng3, ln1g, ln1b, ln2g, ln2b, bdw = (vec(a) for a in (sgu_g, gla_norm_g, ln1_g, ln1_b, ln2_g, ln2_b, b_dw))

    streams = [
        dict(x=x_prompt.reshape(bp * tp, d), b=bp, t=tp, latent=False, state=None, mod_row=lambda i: bs),
        dict(x=x_sample.reshape(bs * ts, d), b=bs, t=ts, latent=True, state=state_gla.astype(F32),
             mod_row=lambda i: (i * 1024) // ts),
    ]
    new_states = []
    for s in streams:
        x, f = s["x"].astype(F32), None
        tables = tuple(jnp.asarray(a).astype(BF16) for a in _dft_tables(s["t"]))
        for l in range(depth):
            prev_ln = (ln2g, ln2b) if l else None
            x2, z, lr = _in_call(x, f, mod, l, s["mod_row"], prev_ln, w_gm, w_main, w_lr, alpha)
            x = x2 if l else x
            ya = _fft_call(z, l, s["b"], s["t"], tables, w_fft_out_b, bm)
            yb = _sgu_call(z, l, sgu_g3, sgu_ws_b, sgu_bias, w_sgu_out_b, bm)
            yc, st = _gla_call(z, lr, l, s["b"], s["t"], s["state"], w2h, b2h, ng3, w_gla_out_b, bm,
                               want_state=not s["latent"])
            if st is not None:
                new_states.append(st)
            x, h2 = _mix_call(ya, yb, yc, x, mod, l, s["mod_row"], w_o_b, ln1g, ln1b, alpha)
            f = _ffn_call(h2, l, s["t"], s["latent"], w_up_b, w_dw9, bdw, w_down_b)
        s["y"] = _ln2_call(x, f, mod, depth - 1, s["mod_row"], ln2g, ln2b, alpha)

    y_prompt = streams[0]["y"].reshape(bp, tp, d).astype(x_prompt.dtype)
    y_sample = streams[1]["y"].reshape(bs, ts, d).astype(x_sample.dtype)
    new_state_gla = jnp.stack(new_states, axis=1).astype(x_prompt.dtype)
    return (y_prompt, y_sample, new_state_gla)
```

```python
import functools
import math

import numpy as np
import jax
import jax.numpy as jnp
from jax import lax
from jax.experimental import pallas as pl
from jax.experimental.pallas import tpu as pltpu

F32 = jnp.float32
BF16 = jnp.bfloat16

D_MODEL = 1024
GRID_W = 64
FFT_GROUPS = 4
FFT_GC = 128
FFT_W = FFT_GROUPS * FFT_GC
SGU_GROUPS = 4
SGU_GC = 128
SGU_W = SGU_GROUPS * SGU_GC
SGU_CHUNK = 128
GLA_H = 4
GLA_DK = 128
GLA_DV = 256
GLA_LR = 16
GLA_GATE_NORM = 16.0
GLA_CHUNK = 64
FFT_FLIP_ROWS = 256
FFT_OUT_ROWS = 512
GLA_BLOCK_ROWS = 2048
GLA_GROUP_ROWS = 256
D_FF = 2816
N_BRANCH = 3
N_ADA = 6
LN_EPS = 1e-5
LR_PAD = 128
IN_CHUNK_ROWS = 256
FFN_TILE_COLS = 256
FFN_CHUNK_ROWS = 256

COL_VV = 0
COL_R = COL_VV + GLA_H * GLA_DV
COL_GM = COL_R + GLA_H * GLA_DV
COL_AF = COL_GM + N_BRANCH * D_MODEL
COL_U = COL_AF + FFT_W
COL_V = COL_U + SGU_W
COL_Q = COL_V + SGU_W
COL_K = COL_Q + GLA_H * GLA_DK
Z_W = COL_K + GLA_H * GLA_DK
IN_TN = 1536

V7X_VMEM_BYTES = 64 * 1024 * 1024
VMEM_LIMIT = V7X_VMEM_BYTES - 8 * 1024 * 1024


def _mm(a, b):
    return jnp.dot(a, b, preferred_element_type=F32)


def _sigmoid(x):
    return 0.5 * jnp.tanh(0.5 * x) + 0.5


def _silu(x):
    return x * _sigmoid(x)


def _gelu(x):
    c = math.sqrt(2.0 / math.pi)
    half = 0.5 * x
    return half + half * jnp.tanh(x * (c + (c * 0.044715) * (x * x)))


def _norm(x):
    mu = jnp.mean(x, axis=-1, keepdims=True)
    xc = x - mu
    var = jnp.mean(xc * xc, axis=-1, keepdims=True)
    return xc * lax.rsqrt(var + LN_EPS)


def _params(sem, limit=VMEM_LIMIT, flags=None):
    return pltpu.CompilerParams(dimension_semantics=sem, vmem_limit_bytes=limit, flags=flags)


def _ada_kernel(c_ref, w_ref, b_ref, o_ref):
    s = _silu(c_ref[...]).astype(BF16)
    o_ref[...] = _mm(s, w_ref[...].astype(BF16)) + b_ref[...]


def _ada_call(cond, w_ada, b_ada):
    depth, d, n = w_ada.shape
    rows = cond.shape[0]
    tn = 1536
    return pl.pallas_call(
        _ada_kernel,
        grid=(depth, n // tn),
        in_specs=[
            pl.BlockSpec((rows, d), lambda l, j: (0, 0)),
            pl.BlockSpec((None, d, tn), lambda l, j: (l, 0, j)),
            pl.BlockSpec((None, 1, tn), lambda l, j: (l, 0, j)),
        ],
        out_specs=pl.BlockSpec((None, rows, tn), lambda l, j: (l, 0, j)),
        out_shape=jax.ShapeDtypeStruct((depth, rows, n), F32),
        compiler_params=_params(("arbitrary", "arbitrary")),
        name="ada",
    )(cond, w_ada, b_ada.reshape(depth, 1, n))


def _in_kernel(*refs, alpha, has_prev):
    if has_prev:
        (x_ref, f_ref, pmod_ref, pg_ref, pb_ref, mod_ref, w_ref, wlr_ref,
         z_ref, lr_ref, x2_ref, h_ref) = refs
    else:
        x_ref, mod_ref, w_ref, wlr_ref, z_ref, lr_ref, h_ref = refs

    @pl.when(pl.program_id(1) == 0)
    def _():
        cr = IN_CHUNK_ROWS
        for c in range(x_ref.shape[0] // cr):
            rows = slice(c * cr, (c + 1) * cr)
            x = x_ref[rows, :]
            if has_prev:
                g2 = pmod_ref[5:6, :]
                x = _norm(alpha * x + g2 * f_ref[rows, :]) * pg_ref[...] + pb_ref[...]
                x2_ref[rows, :] = x
            h = (_norm(x) * (1.0 + mod_ref[1:2, :]) + mod_ref[0:1, :]).astype(BF16)
            h_ref[rows, :] = h
            lr_ref[rows, :] = _mm(h, wlr_ref[...])
            z_ref[rows, :] = _mm(h, w_ref[...]).astype(BF16)

    @pl.when(pl.program_id(1) != 0)
    def _():
        z_ref[...] = _mm(h_ref[...], w_ref[...]).astype(BF16)


def _in_call(x, f, mod, layer, mod_row, prev_ln, w_main, w_lr, alpha):
    m, d = x.shape
    tm = 1024
    tn = IN_TN
    has_prev = f is not None
    row = lambda i, j: (i, 0)
    in_specs = [pl.BlockSpec((tm, d), row)]
    args = [x]
    if has_prev:
        pg, pb = prev_ln
        in_specs += [
            pl.BlockSpec((tm, d), row),
            pl.BlockSpec((None, None, N_ADA, d), lambda i, j: (layer - 1, mod_row(i), 0, 0)),
            pl.BlockSpec((None, 1, d), lambda i, j: (layer - 1, 0, 0)),
            pl.BlockSpec((None, 1, d), lambda i, j: (layer - 1, 0, 0)),
        ]
        args += [f, mod, pg, pb]
    in_specs += [
        pl.BlockSpec((None, None, N_ADA, d), lambda i, j: (layer, mod_row(i), 0, 0)),
        pl.BlockSpec((None, d, tn), lambda i, j: (layer, 0, j)),
        pl.BlockSpec((None, d, LR_PAD), lambda i, j: (layer, 0, 0)),
    ]
    args += [mod, w_main, w_lr]
    out_specs = [pl.BlockSpec((tm, tn), lambda i, j: (i, j)), pl.BlockSpec((tm, LR_PAD), row)]
    out_shape = [jax.ShapeDtypeStruct((m, Z_W), BF16), jax.ShapeDtypeStruct((m, LR_PAD), F32)]
    if has_prev:
        out_specs.append(pl.BlockSpec((tm, d), row))
        out_shape.append(jax.ShapeDtypeStruct((m, d), F32))
    outs = pl.pallas_call(
        functools.partial(_in_kernel, alpha=alpha, has_prev=has_prev),
        grid=(m // tm, Z_W // tn),
        in_specs=in_specs,
        out_specs=out_specs,
        out_shape=out_shape,
        scratch_shapes=[pltpu.VMEM((tm, d), BF16)],
        compiler_params=_params(("arbitrary", "arbitrary")),
        name="in_proj",
    )(*args)
    if has_prev:
        z, lr, x2 = outs
        return x2, z, lr
    z, lr = outs
    return None, z, lr


@functools.lru_cache(maxsize=None)
def _dft_tables(t):
    def cs(rows, n):
        ang = (2.0 * np.pi / n) * ((np.arange(rows)[:, None] * np.arange(n)[None, :]) % n).astype(np.float64)
        return np.cos(ang).astype(np.float32), np.sin(ang).astype(np.float32)

    ct, st = cs(t // 2 + 8, t)
    cc, sc = cs(FFT_GC, FFT_GC)
    flip = np.eye(min(FFT_FLIP_ROWS, t // 2), dtype=np.float32)[::-1].copy()
    return ct, st, cc, sc, flip


def _fft_kernel(a_ref, gm_ref, ct_ref, st_ref, cc_ref, sc_ref, flip_ref, w_ref, bm_ref, y_ref,
                xc_ref, xs_ref, af_ref, *, t):
    half = t // 2
    fb = flip_ref.shape[0]
    for g in range(FFT_GROUPS):
        cols = slice(g * FFT_GC, (g + 1) * FFT_GC)
        xg = a_ref[:, cols]
        xc_ref[:, cols] = _mm(xg, cc_ref[...]).astype(BF16)
        xs_ref[:, cols] = _mm(xg, sc_ref[...]).astype(BF16)
    scale = 1.0 / math.sqrt(t * FFT_GC)
    p = _mm(ct_ref[...], xc_ref[...]) * scale
    q = _mm(st_ref[...], xs_ref[...]) * scale
    af_ref[0:half, :] = (p - q)[0:half].astype(BF16)
    mirrored = (p + q)[1:half + 1].astype(BF16)
    for i in range(half // fb):
        block = mirrored[half - (i + 1) * fb:half - i * fb]
        af_ref[half + i * fb:half + (i + 1) * fb, :] = _mm(flip_ref[...], block).astype(BF16)
    tr = min(t, FFT_OUT_ROWS)
    for r in range(t // tr):
        rows = slice(r * tr, (r + 1) * tr)
        gate = _sigmoid(gm_ref[rows, :].astype(F32) + bm_ref[...])
        y_ref[rows, :] = (gate * _mm(af_ref[rows, :], w_ref[...])).astype(BF16)


def _fft_call(z, layer, b, t, tables, w_fft_out, b_merge):
    m = z.shape[0]
    ct, st, cc, sc, flip = tables
    const = lambda a: pl.BlockSpec(a.shape, lambda i: (0, 0), pipeline_mode=pl.Buffered(1))
    return pl.pallas_call(
        functools.partial(_fft_kernel, t=t),
        grid=(b,),
        in_specs=[
            pl.BlockSpec((t, FFT_W), lambda i: (i, COL_AF // FFT_W)),
            pl.BlockSpec((t, D_MODEL), lambda i: (i, COL_GM // D_MODEL)),
            const(ct), const(st), const(cc), const(sc), const(flip),
            pl.BlockSpec((None, FFT_W, D_MODEL), lambda i: (layer, 0, 0)),
            pl.BlockSpec((None, None, 1, D_MODEL), lambda i: (layer, 0, 0, 0)),
        ],
        out_specs=pl.BlockSpec((t, D_MODEL), lambda i: (i, 0)),
        out_shape=jax.ShapeDtypeStruct((m, D_MODEL), BF16),
        scratch_shapes=[pltpu.VMEM((t, FFT_W), BF16)] * 3,
        compiler_params=_params(("arbitrary",)),
        name="fft",
    )(z, z, ct, st, cc, sc, flip, w_fft_out, b_merge)


def _sgu_kernel(u_ref, v_ref, gm_ref, g_ref, ws_ref, bias_ref, w_ref, bm_ref, y_ref, vm_ref):
    tm = u_ref.shape[0]
    v = _gelu(v_ref[...].astype(F32))
    vn = (_norm(v) * g_ref[...]).astype(BF16)
    for n in range(tm // SGU_CHUNK):
        rows = slice(n * SGU_CHUNK, (n + 1) * SGU_CHUNK)
        for g in range(SGU_GROUPS):
            cols = slice(g * SGU_GC, (g + 1) * SGU_GC)
            vm_ref[rows, cols] = _mm(ws_ref[g], vn[rows, cols]) + bias_ref[:, cols]
    u = _gelu(u_ref[...].astype(F32))
    yb = _mm((u * vm_ref[...]).astype(BF16), w_ref[...])
    gate = _sigmoid(gm_ref[...].astype(F32) + bm_ref[...])
    y_ref[...] = (gate * yb).astype(BF16)


def _sgu_call(z, layer, sgu_g, sgu_ws, sgu_bias, w_sgu_out, b_merge):
    m = z.shape[0]
    tm = 1024
    return pl.pallas_call(
        _sgu_kernel,
        grid=(m // tm,),
        in_specs=[
            pl.BlockSpec((tm, SGU_W), lambda i: (i, COL_U // SGU_W)),
            pl.BlockSpec((tm, SGU_W), lambda i: (i, COL_V // SGU_W)),
            pl.BlockSpec((tm, D_MODEL), lambda i: (i, COL_GM // D_MODEL + 1)),
            pl.BlockSpec((None, 1, SGU_W), lambda i: (layer, 0, 0)),
            pl.BlockSpec((None, SGU_GROUPS, SGU_CHUNK, SGU_CHUNK), lambda i: (layer, 0, 0, 0)),
            pl.BlockSpec((None, SGU_CHUNK, SGU_W), lambda i: (layer, 0, 0)),
            pl.BlockSpec((None, SGU_W, D_MODEL), lambda i: (layer, 0, 0)),
            pl.BlockSpec((None, None, 1, D_MODEL), lambda i: (layer, 1, 0, 0)),
        ],
        out_specs=pl.BlockSpec((tm, D_MODEL), lambda i: (i, 0)),
        out_shape=jax.ShapeDtypeStruct((m, D_MODEL), BF16),
        scratch_shapes=[pltpu.VMEM((tm, SGU_W), F32)],
        compiler_params=_params(("arbitrary",)),
        name="sgu",
    )(z, z, z, sgu_g, sgu_ws, sgu_bias, w_sgu_out, b_merge)


def _gla_kernel(*refs, t, seq, has_s0, want_state):
    refs = list(refs)
    q_ref, k_ref, v_ref, r_ref, lr_ref, w2_ref, b2_ref = refs[:7]
    del refs[:7]
    s0_ref = refs.pop(0) if has_s0 else None
    ng_ref, wout_ref, gm_ref, bm_ref, y_ref = refs[:5]
    del refs[:5]
    sfin_ref = refs.pop(0) if want_state else None
    oall_ref, o_ref, lg_ref, qg_ref, kg_ref, ds_ref, ss_ref, gtot_ref = refs

    head = pl.program_id(1)
    ch = GLA_CHUNK
    n = t // ch
    gr = GLA_GROUP_ROWS
    cpg = gr // ch
    n_groups = t // gr
    unroll = min(n_groups, 8)
    nt = (((1,), (1,)), ((), ()))
    tn = (((0,), (0,)), ((), ()))

    lrb = lr_ref[...].astype(BF16)
    for d in range(2):
        zg = _mm(lrb, w2_ref[d]) + b2_ref[d]
        lg_ref[d] = (jnp.minimum(zg, 0.0) - jnp.log(1.0 + jnp.exp(-jnp.abs(zg)))) * (1.0 / GLA_GATE_NORM)

    def split3(x):
        hi = x.astype(BF16)
        rem = x - hi.astype(F32)
        mid = rem.astype(BF16)
        lo = (rem - mid.astype(F32)).astype(BF16)
        return hi, mid, lo

    row = lax.broadcasted_iota(jnp.int32, (gr, gr), 0)
    col = lax.broadcasted_iota(jnp.int32, (gr, gr), 1)
    same = (row // ch) == (col // ch)
    incl = (same & (col <= row), same & (col >= row))
    cum_mats = tuple(m.astype(BF16) for m in incl)
    brow = lax.broadcasted_iota(jnp.int32, (gr, cpg * GLA_DK), 0)
    bcol = lax.broadcasted_iota(jnp.int32, (gr, cpg * GLA_DK), 1)
    own_chunk = (brow // ch) == (bcol // GLA_DK)

    def spread(x):
        return jnp.where(own_chunk, jnp.concatenate([x] * cpg, axis=1), jnp.zeros((), x.dtype))

    if gtot_ref.shape[0] > n * 8:
        gtot_ref[...] = jnp.zeros_like(gtot_ref)

    for d in range(2):
        def prepare(g, carry, d=d):
            r0 = pl.multiple_of(g * gr, gr)
            lg3 = jnp.concatenate(split3(lg_ref[d, pl.ds(r0, gr), :]), axis=1)
            cums = _mm(cum_mats[d], lg3)
            gc = cums[:, 0:GLA_DK] + cums[:, GLA_DK:2 * GLA_DK] + cums[:, 2 * GLA_DK:3 * GLA_DK]
            g4 = gc.reshape(cpg, ch, GLA_DK)
            tot = g4[:, ch - 1:ch, :] if d == 0 else g4[:, 0:1, :]
            rc = (tot - g4).reshape(gr, GLA_DK)
            t0 = pl.multiple_of(g * (cpg * 8), cpg * 8)
            gtot_ref[pl.ds(t0, cpg * 8), :] = jnp.broadcast_to(tot, (cpg, 8, GLA_DK)).reshape(cpg * 8, GLA_DK)
            q = q_ref[pl.ds(r0, gr), :].astype(F32) * (GLA_DK ** -0.5)
            k = k_ref[pl.ds(r0, gr), :].astype(F32)
            qg_ref[pl.ds(r0, gr), :] = (q * jnp.exp(gc)).astype(BF16)
            kg_ref[pl.ds(r0, gr), :] = (k * jnp.exp(-gc)).astype(BF16)
            kd = (k * jnp.exp(rc)).astype(BF16)
            d0 = pl.multiple_of(g * (cpg * GLA_DK), cpg * GLA_DK)
            ds_ref[pl.ds(d0, cpg * GLA_DK), :] = lax.dot_general(
                spread(kd), v_ref[pl.ds(r0, gr), :], tn, preferred_element_type=F32)
            return carry

        lax.fori_loop(0, n_groups, prepare, 0, unroll=unroll)

        dec = jnp.exp(gtot_ref[...].T)
        cps = seq // ch
        first_c, last_c = (0, cps - 1) if d == 0 else (cps - 1, 0)
        for c in (range(n) if d == 0 else range(n - 1, -1, -1)):
            if c % cps == first_c:
                s = s0_ref[d] if has_s0 else jnp.zeros((GLA_DK, GLA_DV), F32)
            ss_ref[c * GLA_DK:(c + 1) * GLA_DK, :] = s.astype(BF16)
            s = s * dec[:, 8 * c:8 * c + 1] + ds_ref[c * GLA_DK:(c + 1) * GLA_DK, :]
            if want_state and c % cps == last_c:
                sfin_ref[c // cps, d] = s

        def attend(g, carry, d=d):
            r0 = pl.multiple_of(g * gr, gr)
            d0 = pl.multiple_of(g * (cpg * GLA_DK), cpg * GLA_DK)
            qg = qg_ref[pl.ds(r0, gr), :]
            a = lax.dot_general(qg, kg_ref[pl.ds(r0, gr), :], nt, preferred_element_type=F32)
            a = jnp.where(incl[d], a, 0.0).astype(BF16)
            o = _mm(a, v_ref[pl.ds(r0, gr), :]) + _mm(spread(qg), ss_ref[pl.ds(d0, cpg * GLA_DK), :])
            if d == 0:
                o_ref[pl.ds(r0, gr), :] = o
            else:
                o_ref[pl.ds(r0, gr), :] += o
            return carry

        lax.fori_loop(0, n_groups, attend, 0, unroll=unroll)

    o = o_ref[...]
    o = o * lax.rsqrt(jnp.mean(o * o, axis=-1, keepdims=True) + LN_EPS) * ng_ref[...]
    oall_ref[head] = (o * _silu(r_ref[...].astype(F32))).astype(BF16)

    @pl.when(head == pl.num_programs(1) - 1)
    def _():
        o_heads = jnp.concatenate([oall_ref[h] for h in range(GLA_H)], axis=1)
        gate = _sigmoid(gm_ref[...].astype(F32) + bm_ref[...])
        y_ref[...] = (gate * _mm(o_heads, wout_ref[...])).astype(BF16)


def _gla_call(z, lr, layer, b, seq, state, w2h, b2h, gla_norm_g, w_gla_out, b_merge, want_state):
    m = z.shape[0]
    has_s0 = state is not None
    t = seq if has_s0 else max(seq, min(GLA_BLOCK_ROWS, m))
    n_seq = t // seq
    assert t % seq == 0 and m % t == 0 and seq % GLA_GROUP_ROWS == 0
    n_chunks = t // GLA_CHUNK
    in_specs = [
        pl.BlockSpec((t, GLA_DK), lambda i, h: (i, COL_Q // GLA_DK + h)),
        pl.BlockSpec((t, GLA_DK), lambda i, h: (i, COL_K // GLA_DK + h)),
        pl.BlockSpec((t, GLA_DV), lambda i, h: (i, COL_VV // GLA_DV + h)),
        pl.BlockSpec((t, GLA_DV), lambda i, h: (i, COL_R // GLA_DV + h)),
        pl.BlockSpec((t, LR_PAD), lambda i, h: (i, 0)),
        pl.BlockSpec((None, None, 2, LR_PAD, GLA_DK), lambda i, h: (layer, h, 0, 0, 0)),
        pl.BlockSpec((None, None, 2, 1, GLA_DK), lambda i, h: (layer, h, 0, 0, 0)),
    ]
    args = [z, z, z, z, lr, w2h, b2h]
    if has_s0:
        in_specs.append(pl.BlockSpec((None, None, 2, None, GLA_DK, GLA_DV), lambda i, h: (i, layer, 0, h, 0, 0)))
        args.append(state)
    in_specs += [
        pl.BlockSpec((None, 1, GLA_DV), lambda i, h: (layer, 0, 0)),
        pl.BlockSpec((None, GLA_H * GLA_DV, D_MODEL), lambda i, h: (layer, 0, 0)),
        pl.BlockSpec((t, D_MODEL), lambda i, h: (i, COL_GM // D_MODEL + 2)),
        pl.BlockSpec((None, None, 1, D_MODEL), lambda i, h: (layer, 2, 0, 0)),
    ]
    args += [gla_norm_g, w_gla_out, z, b_merge]
    out_specs = [pl.BlockSpec((t, D_MODEL), lambda i, h: (i, 0))]
    out_shape = [jax.ShapeDtypeStruct((m, D_MODEL), BF16)]
    if want_state:
        out_specs.append(pl.BlockSpec((n_seq, 2, None, GLA_DK, GLA_DV), lambda i, h: (i, 0, h, 0, 0)))
        out_shape.append(jax.ShapeDtypeStruct((b, 2, GLA_H, GLA_DK, GLA_DV), F32))
    outs = pl.pallas_call(
        functools.partial(_gla_kernel, t=t, seq=seq, has_s0=has_s0, want_state=want_state),
        grid=(m // t, GLA_H),
        in_specs=in_specs,
        out_specs=out_specs,
        out_shape=out_shape,
        scratch_shapes=[
            pltpu.VMEM((GLA_H, t, GLA_DV), BF16),
            pltpu.VMEM((t, GLA_DV), F32),
            pltpu.VMEM((2, t, GLA_DK), F32),
            pltpu.VMEM((t, GLA_DK), BF16),
            pltpu.VMEM((t, GLA_DK), BF16),
            pltpu.VMEM((n_chunks * GLA_DK, GLA_DV), F32),
            pltpu.VMEM((n_chunks * GLA_DK, GLA_DV), BF16),
            pltpu.VMEM((max(n_chunks * 8, GLA_DK), GLA_DK), F32),
        ],
        compiler_params=_params(("arbitrary", "arbitrary")),
        name="gla",
    )(*args)
    return (outs[0], outs[1]) if want_state else (outs[0], None)


def _mix_kernel(ya_ref, yb_ref, yc_ref, x_ref, mod_ref, wo_ref, g_ref, b_ref, x1_ref, h2_ref, *, alpha):
    msum = (ya_ref[...].astype(F32) + yb_ref[...].astype(F32) + yc_ref[...].astype(F32)).astype(BF16)
    mix = _mm(msum, wo_ref[...])
    x1 = _norm(alpha * x_ref[...] + mod_ref[2:3, :] * mix) * g_ref[...] + b_ref[...]
    x1_ref[...] = x1
    h2_ref[...] = (_norm(x1) * (1.0 + mod_ref[4:5, :]) + mod_ref[3:4, :]).astype(BF16)


def _mix_call(ya, yb, yc, x, mod, layer, mod_row, w_o, ln1_g, ln1_b, alpha):
    m, d = x.shape
    tm = 512
    row = lambda i: (i, 0)
    vec = pl.BlockSpec((None, 1, d), lambda i: (layer, 0, 0))
    return pl.pallas_call(
        functools.partial(_mix_kernel, alpha=alpha),
        grid=(m // tm,),
        in_specs=[
            pl.BlockSpec((tm, d), row), pl.BlockSpec((tm, d), row), pl.BlockSpec((tm, d), row),
            pl.BlockSpec((tm, d), row),
            pl.BlockSpec((None, None, N_ADA, d), lambda i: (layer, mod_row((i * tm) // 1024), 0, 0)),
            pl.BlockSpec((None, d, d), lambda i: (layer, 0, 0)),
            vec, vec,
        ],
        out_specs=[pl.BlockSpec((tm, d), row), pl.BlockSpec((tm, d), row)],
        out_shape=[jax.ShapeDtypeStruct((m, d), F32), jax.ShapeDtypeStruct((m, d), BF16)],
        compiler_params=_params(("arbitrary",)),
        name="mix",
    )(ya, yb, yc, x, mod, w_o, ln1_g, ln1_b)


def _ffn_kernel(h2_ref, wa0_ref, wg0_ref, wa_ref, wg_ref, wdw_ref, bdw_ref, wd_ref, f_ref,
                a_even, g_even, a_odd, g_odd, wm_ref, *, seq, latent):
    k = pl.program_id(1)
    nk = pl.num_programs(1)
    n = pl.program_id(0) * nk + k

    rows, tc = g_even.shape
    cr = FFN_CHUNK_ROWS
    n_chunks = rows // cr
    period = GRID_W if latent else seq
    pad = GRID_W if latent else 0
    nl = cr // period

    def set_taps(wdw):
        pos = lax.broadcasted_iota(jnp.int32, (period, tc), 0)
        for i in (range(9) if latent else range(3, 6)):
            wi = jnp.broadcast_to(wdw[i:i + 1, :], (period, tc))
            if i % 3 == 0:
                wi = jnp.where(pos > 0, wi, 0.0)
            if i % 3 == 2:
                wi = jnp.where(pos < period - 1, wi, 0.0)
            wm_ref[i] = wi

    def up(r0, wa, wg, a_dst, g_dst):
        h2 = h2_ref[pl.ds(r0, cr), :]
        a_dst[pl.ds(r0 + pad, cr), :] = _mm(h2, wa[...])
        g_dst[pl.ds(r0, cr), :] = _mm(h2, wg[...])

    def conv_gate(r0, a_src, g_src):
        w = lambda i: wm_ref[i].reshape(1, period, tc)
        slab = a_src[pl.ds(r0, cr + 2 * pad), :].reshape((cr + 2 * pad) // period, period, tc)
        left = pltpu.roll(slab, 1, axis=1)
        right = pltpu.roll(slab, period - 1, axis=1)
        if latent:
            conv = sum(w(3 * dh) * left[dh:dh + nl] + w(3 * dh + 1) * slab[dh:dh + nl]
                       + w(3 * dh + 2) * right[dh:dh + nl] for dh in range(3))
        else:
            conv = w(3) * left + w(4) * slab + w(5) * right
        conv = conv.reshape(cr, tc) + bdw_ref[...]
        return (_gelu(conv) * g_src[pl.ds(r0, cr), :]).astype(BF16)

    def chunks(body, unroll):
        lax.fori_loop(0, n_chunks, lambda c, carry: (body(pl.multiple_of(c * cr, cr)), carry)[1], 0,
                      unroll=unroll)

    @pl.when(n == 0)
    def _():
        if pad:
            for a_ref in (a_even, a_odd):
                a_ref[0:pad, :] = jnp.zeros((pad, tc), F32)
                a_ref[pad + rows:pad + rows + pad, :] = jnp.zeros((pad, tc), F32)
        chunks(lambda r0: up(r0, wa0_ref, wg0_ref, a_even, g_even), 2)

    @pl.when(k == 0)
    def _():
        f_ref[...] = jnp.zeros_like(f_ref)

    def step(a_cur, g_cur, a_nxt, g_nxt):
        set_taps(wdw_ref)

        def body(r0):
            f_ref[pl.ds(r0, cr), :] += _mm(conv_gate(r0, a_cur, g_cur), wd_ref[...])
            up(r0, wa_ref, wg_ref, a_nxt, g_nxt)

        chunks(body, 8)

    @pl.when(n % 2 == 0)
    def _():
        step(a_even, g_even, a_odd, g_odd)

    @pl.when(n % 2 == 1)
    def _():
        step(a_odd, g_odd, a_even, g_even)


def _ffn_call(h2, layer, seq, latent, w_up, w_dw, b_dw, w_down):
    m, d = h2.shape
    rows = 2048
    tc = FFN_TILE_COLS
    nk = D_FF // tc
    nb = m // rows
    if latent:
        assert rows == seq and FFN_CHUNK_ROWS % GRID_W == 0
    else:
        assert FFN_CHUNK_ROWS % seq == 0
    pad = GRID_W if latent else 0
    period = GRID_W if latent else seq
    up_block = lambda i, k: jnp.minimum((i * nk + k + 1) // nk, nb - 1)
    up_tile = lambda k: (k + 1) % nk
    return pl.pallas_call(
        functools.partial(_ffn_kernel, seq=seq, latent=latent),
        grid=(nb, nk),
        in_specs=[
            pl.BlockSpec((rows, d), lambda i, k: (up_block(i, k), 0)),
            pl.BlockSpec((None, None, d, tc), lambda i, k: (layer, 0, 0, 0)),
            pl.BlockSpec((None, None, d, tc), lambda i, k: (layer, nk, 0, 0)),
            pl.BlockSpec((None, None, d, tc), lambda i, k: (layer, up_tile(k), 0, 0)),
            pl.BlockSpec((None, None, d, tc), lambda i, k: (layer, nk + up_tile(k), 0, 0)),
            pl.BlockSpec((None, 9, tc), lambda i, k: (layer, 0, k)),
            pl.BlockSpec((None, 1, tc), lambda i, k: (layer, 0, k)),
            pl.BlockSpec((None, tc, d), lambda i, k: (layer, k, 0)),
        ],
        out_specs=pl.BlockSpec((rows, d), lambda i, k: (i, 0)),
        out_shape=jax.ShapeDtypeStruct((m, d), F32),
        scratch_shapes=[pltpu.VMEM((rows + 2 * pad, tc), F32), pltpu.VMEM((rows, tc), F32)] * 2
        + [pltpu.VMEM((9, period, tc), F32)],
        compiler_params=_params(("arbitrary", "arbitrary")),
        name="ffn",
    )(h2, w_up, w_up, w_up, w_up, w_dw, b_dw, w_down)


def _ln2_kernel(x_ref, f_ref, mod_ref, g_ref, b_ref, o_ref, *, alpha):
    o_ref[...] = _norm(alpha * x_ref[...] + mod_ref[5:6, :] * f_ref[...]) * g_ref[...] + b_ref[...]


def _ln2_call(x1, f, mod, layer, mod_row, ln2_g, ln2_b, alpha):
    m, d = x1.shape
    tm = 1024
    row = lambda i: (i, 0)
    vec = pl.BlockSpec((None, 1, d), lambda i: (layer, 0, 0))
    return pl.pallas_call(
        functools.partial(_ln2_kernel, alpha=alpha),
        grid=(m // tm,),
        in_specs=[
            pl.BlockSpec((tm, d), row), pl.BlockSpec((tm, d), row),
            pl.BlockSpec((None, None, N_ADA, d), lambda i: (layer, mod_row(i), 0, 0)),
            vec, vec,
        ],
        out_specs=pl.BlockSpec((tm, d), row),
        out_shape=jax.ShapeDtypeStruct((m, d), F32),
        compiler_params=_params(("arbitrary",)),
        name="ln2",
    )(x1, f, mod, ln2_g, ln2_b)


def kernel(x_prompt, x_sample, state_gla, c, c_ctx, w_in, b_merge, w_fft_out, sgu_g, sgu_ws, sgu_b, w_sgu_out, gla_w2, gla_b, gla_norm_g, w_gla_out, w_o, ln1_g, ln1_b, w_ada, b_ada, w_up, w_dw, b_dw, w_down, ln2_g, ln2_b):
    depth = w_in.shape[0]
    alpha = (2.0 * depth) ** 0.25
    bp, tp, d = x_prompt.shape
    bs, ts, _ = x_sample.shape

    n_cond = 16
    cond = jnp.zeros((n_cond, d), F32).at[:bs].set(c.astype(F32)).at[bs].set(c_ctx.astype(F32))
    mod = _ada_call(cond, w_ada, b_ada).reshape(depth, n_cond, N_ADA, d)

    cuts = np.cumsum([FFT_W, SGU_W, SGU_W, GLA_H * GLA_DK, GLA_H * GLA_DK, GLA_H * GLA_DV, GLA_H * GLA_DV, 2 * GLA_LR]).tolist()
    p_af, p_u, p_v, p_q, p_k, p_vv, p_r, p_lr, p_gm = jnp.split(w_in, cuts, axis=-1)
    w_main = jnp.concatenate([p_vv, p_r, p_gm, p_af, p_u, p_v, p_q, p_k], axis=-1).astype(BF16)
    w_lr = jnp.pad(p_lr, ((0, 0), (0, 0), (0, LR_PAD - 2 * GLA_LR))).astype(BF16)
    w2 = gla_w2.reshape(depth, 2, GLA_LR, GLA_H, GLA_DK).transpose(0, 3, 1, 2, 4)
    w2h = jnp.zeros((depth, GLA_H, 2, LR_PAD, GLA_DK), F32)
    w2h = w2h.at[:, :, 0, 0:GLA_LR].set(w2[:, :, 0]).at[:, :, 1, GLA_LR:2 * GLA_LR].set(w2[:, :, 1]).astype(BF16)
    b2h = gla_b.reshape(depth, 2, GLA_H, 1, GLA_DK).transpose(0, 2, 1, 3, 4).astype(F32)
    sgu_bias = jnp.repeat(sgu_b.transpose(0, 2, 1), SGU_GC, axis=-1).astype(F32)
    vec = lambda a: a.reshape(depth, 1, a.shape[-1]).astype(F32)
    bm = b_merge.reshape(depth, N_BRANCH, 1, d).astype(F32)
    w_fft_out_b, w_sgu_out_b, sgu_ws_b = w_fft_out.astype(BF16), w_sgu_out.astype(BF16), sgu_ws.astype(BF16)
    w_gla_out_b, w_o_b, w_down_b = w_gla_out.astype(BF16), w_o.astype(BF16), w_down.astype(BF16)
    w_up_b = w_up.reshape(depth, d, 2 * D_FF // FFN_TILE_COLS, FFN_TILE_COLS).transpose(0, 2, 1, 3).astype(BF16)
    w_dw9 = w_dw.reshape(depth, 9, D_FF).astype(F32)
    sgu_g3, ng3, ln1g, ln1b, ln2g, ln2b, bdw = (vec(a) for a in (sgu_g, gla_norm_g, ln1_g, ln1_b, ln2_g, ln2_b, b_dw))

    streams = [
        dict(x=x_prompt.reshape(bp * tp, d), b=bp, t=tp, latent=False, state=None, mod_row=lambda i: bs),
        dict(x=x_sample.reshape(bs * ts, d), b=bs, t=ts, latent=True, state=state_gla.astype(F32),
             mod_row=lambda i: (i * 1024) // ts),
    ]
    new_states = []
    for s in streams:
        x, f = s["x"].astype(F32), None
        tables = tuple(jnp.asarray(a).astype(BF16) for a in _dft_tables(s["t"]))
        for l in range(depth):
            prev_ln = (ln2g, ln2b) if l else None
            x2, z, lr = _in_call(x, f, mod, l, s["mod_row"], prev_ln, w_main, w_lr, alpha)
            x = x2 if l else x
            ya = _fft_call(z, l, s["b"], s["t"], tables, w_fft_out_b, bm)
            yb = _sgu_call(z, l, sgu_g3, sgu_ws_b, sgu_bias, w_sgu_out_b, bm)
            yc, st = _gla_call(z, lr, l, s["b"], s["t"], s["state"], w2h, b2h, ng3, w_gla_out_b, bm,
                               want_state=not s["latent"])
            if st is not None:
                new_states.append(st)
            x, h2 = _mix_call(ya, yb, yc, x, mod, l, s["mod_row"], w_o_b, ln1g, ln1b, alpha)
            f = _ffn_call(h2, l, s["t"], s["latent"], w_up_b, w_dw9, bdw, w_down_b)
        s["y"] = _ln2_call(x, f, mod, depth - 1, s["mod_row"], ln2g, ln2b, alpha)

    y_prompt = streams[0]["y"].reshape(bp, tp, d).astype(x_prompt.dtype)
    y_sample = streams[1]["y"].reshape(bs, ts, d).astype(x_sample.dtype)
    new_state_gla = jnp.stack(new_states, axis=1).astype(x_prompt.dtype)
    return (y_prompt, y_sample, new_state_gla)
```

```python
import functools
import math

import numpy as np
import jax
import jax.numpy as jnp
from jax import lax
from jax.experimental import pallas as pl
from jax.experimental.pallas import tpu as pltpu

F32 = jnp.float32
BF16 = jnp.bfloat16

D_MODEL = 1024
GRID_W = 64
FFT_GROUPS = 4
FFT_GC = 128
FFT_W = FFT_GROUPS * FFT_GC
SGU_GROUPS = 4
SGU_GC = 128
SGU_W = SGU_GROUPS * SGU_GC
SGU_CHUNK = 128
GLA_H = 4
GLA_DK = 128
GLA_DV = 256
GLA_LR = 16
GLA_GATE_NORM = 16.0
GLA_CHUNK = 64
FFT_FLIP_ROWS = 256
FFT_OUT_ROWS = 512
GLA_BLOCK_ROWS = 2048
GLA_GROUP_ROWS = 256
D_FF = 2816
N_BRANCH = 3
N_ADA = 6
LN_EPS = 1e-5
LR_PAD = 128
IN_CHUNK_ROWS = 256
FFN_TILE_COLS = 256
FFN_CHUNK_ROWS = 512

COL_VV = 0
COL_R = COL_VV + GLA_H * GLA_DV
COL_GM = COL_R + GLA_H * GLA_DV
COL_AF = COL_GM + N_BRANCH * D_MODEL
COL_U = COL_AF + FFT_W
COL_V = COL_U + SGU_W
COL_Q = COL_V + SGU_W
COL_K = COL_Q + GLA_H * GLA_DK
Z_W = COL_K + GLA_H * GLA_DK
IN_TN = 1536

V7X_VMEM_BYTES = 64 * 1024 * 1024
VMEM_LIMIT = V7X_VMEM_BYTES - 8 * 1024 * 1024


def _mm(a, b):
    return jnp.dot(a, b, preferred_element_type=F32)


def _sigmoid(x):
    return 0.5 * jnp.tanh(0.5 * x) + 0.5


def _silu(x):
    return x * _sigmoid(x)


def _gelu(x):
    c = math.sqrt(2.0 / math.pi)
    half = 0.5 * x
    return half + half * jnp.tanh(x * (c + (c * 0.044715) * (x * x)))


def _norm(x):
    mu = jnp.mean(x, axis=-1, keepdims=True)
    xc = x - mu
    var = jnp.mean(xc * xc, axis=-1, keepdims=True)
    return xc * lax.rsqrt(var + LN_EPS)


def _params(sem, limit=VMEM_LIMIT, flags=None):
    return pltpu.CompilerParams(dimension_semantics=sem, vmem_limit_bytes=limit, flags=flags)


def _ada_kernel(c_ref, w_ref, b_ref, o_ref):
    s = _silu(c_ref[...]).astype(BF16)
    o_ref[...] = _mm(s, w_ref[...].astype(BF16)) + b_ref[...]


def _ada_call(cond, w_ada, b_ada):
    depth, d, n = w_ada.shape
    rows = cond.shape[0]
    tn = 1536
    return pl.pallas_call(
        _ada_kernel,
        grid=(depth, n // tn),
        in_specs=[
            pl.BlockSpec((rows, d), lambda l, j: (0, 0)),
            pl.BlockSpec((None, d, tn), lambda l, j: (l, 0, j)),
            pl.BlockSpec((None, 1, tn), lambda l, j: (l, 0, j)),
        ],
        out_specs=pl.BlockSpec((None, rows, tn), lambda l, j: (l, 0, j)),
        out_shape=jax.ShapeDtypeStruct((depth, rows, n), F32),
        compiler_params=_params(("arbitrary", "arbitrary")),
        name="ada",
    )(cond, w_ada, b_ada.reshape(depth, 1, n))


def _in_kernel(*refs, alpha, has_prev):
    if has_prev:
        (x_ref, f_ref, pmod_ref, pg_ref, pb_ref, mod_ref, w_ref, wlr_ref,
         z_ref, lr_ref, x2_ref, h_ref) = refs
    else:
        x_ref, mod_ref, w_ref, wlr_ref, z_ref, lr_ref, h_ref = refs

    @pl.when(pl.program_id(1) == 0)
    def _():
        cr = IN_CHUNK_ROWS
        for c in range(x_ref.shape[0] // cr):
            rows = slice(c * cr, (c + 1) * cr)
            x = x_ref[rows, :]
            if has_prev:
                g2 = pmod_ref[5:6, :]
                x = _norm(alpha * x + g2 * f_ref[rows, :]) * pg_ref[...] + pb_ref[...]
                x2_ref[rows, :] = x
            h = (_norm(x) * (1.0 + mod_ref[1:2, :]) + mod_ref[0:1, :]).astype(BF16)
            h_ref[rows, :] = h
            lr_ref[rows, :] = _mm(h, wlr_ref[...])
            z_ref[rows, :] = _mm(h, w_ref[...]).astype(BF16)

    @pl.when(pl.program_id(1) != 0)
    def _():
        z_ref[...] = _mm(h_ref[...], w_ref[...]).astype(BF16)


def _in_call(x, f, mod, layer, mod_row, prev_ln, w_main, w_lr, alpha):
    m, d = x.shape
    tm = 1024
    tn = IN_TN
    has_prev = f is not None
    row = lambda i, j: (i, 0)
    in_specs = [pl.BlockSpec((tm, d), row)]
    args = [x]
    if has_prev:
        pg, pb = prev_ln
        in_specs += [
            pl.BlockSpec((tm, d), row),
            pl.BlockSpec((None, None, N_ADA, d), lambda i, j: (layer - 1, mod_row(i), 0, 0)),
            pl.BlockSpec((None, 1, d), lambda i, j: (layer - 1, 0, 0)),
            pl.BlockSpec((None, 1, d), lambda i, j: (layer - 1, 0, 0)),
        ]
        args += [f, mod, pg, pb]
    in_specs += [
        pl.BlockSpec((None, None, N_ADA, d), lambda i, j: (layer, mod_row(i), 0, 0)),
        pl.BlockSpec((None, d, tn), lambda i, j: (layer, 0, j)),
        pl.BlockSpec((None, d, LR_PAD), lambda i, j: (layer, 0, 0)),
    ]
    args += [mod, w_main, w_lr]
    out_specs = [pl.BlockSpec((tm, tn), lambda i, j: (i, j)), pl.BlockSpec((tm, LR_PAD), row)]
    out_shape = [jax.ShapeDtypeStruct((m, Z_W), BF16), jax.ShapeDtypeStruct((m, LR_PAD), F32)]
    if has_prev:
        out_specs.append(pl.BlockSpec((tm, d), row))
        out_shape.append(jax.ShapeDtypeStruct((m, d), F32))
    outs = pl.pallas_call(
        functools.partial(_in_kernel, alpha=alpha, has_prev=has_prev),
        grid=(m // tm, Z_W // tn),
        in_specs=in_specs,
        out_specs=out_specs,
        out_shape=out_shape,
        scratch_shapes=[pltpu.VMEM((tm, d), BF16)],
        compiler_params=_params(("arbitrary", "arbitrary")),
        name="in_proj",
    )(*args)
    if has_prev:
        z, lr, x2 = outs
        return x2, z, lr
    z, lr = outs
    return None, z, lr


@functools.lru_cache(maxsize=None)
def _dft_tables(t):
    def cs(rows, n):
        ang = (2.0 * np.pi / n) * ((np.arange(rows)[:, None] * np.arange(n)[None, :]) % n).astype(np.float64)
        return np.cos(ang).astype(np.float32), np.sin(ang).astype(np.float32)

    ct, st = cs(t // 2 + 8, t)
    cc, sc = cs(FFT_GC, FFT_GC)
    flip = np.eye(min(FFT_FLIP_ROWS, t // 2), dtype=np.float32)[::-1].copy()
    return ct, st, cc, sc, flip


def _fft_kernel(a_ref, gm_ref, ct_ref, st_ref, cc_ref, sc_ref, flip_ref, w_ref, bm_ref, y_ref,
                xc_ref, xs_ref, af_ref, *, t):
    half = t // 2
    fb = flip_ref.shape[0]
    for g in range(FFT_GROUPS):
        cols = slice(g * FFT_GC, (g + 1) * FFT_GC)
        xg = a_ref[:, cols]
        xc_ref[:, cols] = _mm(xg, cc_ref[...]).astype(BF16)
        xs_ref[:, cols] = _mm(xg, sc_ref[...]).astype(BF16)
    scale = 1.0 / math.sqrt(t * FFT_GC)
    p = _mm(ct_ref[...], xc_ref[...]) * scale
    q = _mm(st_ref[...], xs_ref[...]) * scale
    af_ref[0:half, :] = (p - q)[0:half].astype(BF16)
    mirrored = (p + q)[1:half + 1].astype(BF16)
    for i in range(half // fb):
        block = mirrored[half - (i + 1) * fb:half - i * fb]
        af_ref[half + i * fb:half + (i + 1) * fb, :] = _mm(flip_ref[...], block).astype(BF16)
    tr = min(t, FFT_OUT_ROWS)
    for r in range(t // tr):
        rows = slice(r * tr, (r + 1) * tr)
        gate = _sigmoid(gm_ref[rows, :].astype(F32) + bm_ref[...])
        y_ref[rows, :] = (gate * _mm(af_ref[rows, :], w_ref[...])).astype(BF16)


def _fft_call(z, layer, b, t, tables, w_fft_out, b_merge):
    m = z.shape[0]
    ct, st, cc, sc, flip = tables
    const = lambda a: pl.BlockSpec(a.shape, lambda i: (0, 0), pipeline_mode=pl.Buffered(1))
    return pl.pallas_call(
        functools.partial(_fft_kernel, t=t),
        grid=(b,),
        in_specs=[
            pl.BlockSpec((t, FFT_W), lambda i: (i, COL_AF // FFT_W)),
            pl.BlockSpec((t, D_MODEL), lambda i: (i, COL_GM // D_MODEL)),
            const(ct), const(st), const(cc), const(sc), const(flip),
            pl.BlockSpec((None, FFT_W, D_MODEL), lambda i: (layer, 0, 0)),
            pl.BlockSpec((None, None, 1, D_MODEL), lambda i: (layer, 0, 0, 0)),
        ],
        out_specs=pl.BlockSpec((t, D_MODEL), lambda i: (i, 0)),
        out_shape=jax.ShapeDtypeStruct((m, D_MODEL), BF16),
        scratch_shapes=[pltpu.VMEM((t, FFT_W), BF16)] * 3,
        compiler_params=_params(("arbitrary",)),
        name="fft",
    )(z, z, ct, st, cc, sc, flip, w_fft_out, b_merge)


def _sgu_kernel(u_ref, v_ref, gm_ref, g_ref, ws_ref, bias_ref, w_ref, bm_ref, y_ref, vm_ref):
    tm = u_ref.shape[0]
    v = _gelu(v_ref[...].astype(F32))
    vn = (_norm(v) * g_ref[...]).astype(BF16)
    for n in range(tm // SGU_CHUNK):
        rows = slice(n * SGU_CHUNK, (n + 1) * SGU_CHUNK)
        for g in range(SGU_GROUPS):
            cols = slice(g * SGU_GC, (g + 1) * SGU_GC)
            vm_ref[rows, cols] = _mm(ws_ref[g], vn[rows, cols]) + bias_ref[:, cols]
    u = _gelu(u_ref[...].astype(F32))
    yb = _mm((u * vm_ref[...]).astype(BF16), w_ref[...])
    gate = _sigmoid(gm_ref[...].astype(F32) + bm_ref[...])
    y_ref[...] = (gate * yb).astype(BF16)


def _sgu_call(z, layer, sgu_g, sgu_ws, sgu_bias, w_sgu_out, b_merge):
    m = z.shape[0]
    tm = 1024
    return pl.pallas_call(
        _sgu_kernel,
        grid=(m // tm,),
        in_specs=[
            pl.BlockSpec((tm, SGU_W), lambda i: (i, COL_U // SGU_W)),
            pl.BlockSpec((tm, SGU_W), lambda i: (i, COL_V // SGU_W)),
            pl.BlockSpec((tm, D_MODEL), lambda i: (i, COL_GM // D_MODEL + 1)),
            pl.BlockSpec((None, 1, SGU_W), lambda i: (layer, 0, 0)),
            pl.BlockSpec((None, SGU_GROUPS, SGU_CHUNK, SGU_CHUNK), lambda i: (layer, 0, 0, 0)),
            pl.BlockSpec((None, SGU_CHUNK, SGU_W), lambda i: (layer, 0, 0)),
            pl.BlockSpec((None, SGU_W, D_MODEL), lambda i: (layer, 0, 0)),
            pl.BlockSpec((None, None, 1, D_MODEL), lambda i: (layer, 1, 0, 0)),
        ],
        out_specs=pl.BlockSpec((tm, D_MODEL), lambda i: (i, 0)),
        out_shape=jax.ShapeDtypeStruct((m, D_MODEL), BF16),
        scratch_shapes=[pltpu.VMEM((tm, SGU_W), F32)],
        compiler_params=_params(("arbitrary",)),
        name="sgu",
    )(z, z, z, sgu_g, sgu_ws, sgu_bias, w_sgu_out, b_merge)


def _gla_kernel(*refs, t, seq, has_s0, want_state):
    refs = list(refs)
    q_ref, k_ref, v_ref, r_ref, lr_ref, w2_ref, b2_ref = refs[:7]
    del refs[:7]
    s0_ref = refs.pop(0) if has_s0 else None
    ng_ref, wout_ref, gm_ref, bm_ref, y_ref = refs[:5]
    del refs[:5]
    sfin_ref = refs.pop(0) if want_state else None
    oall_ref, o_ref, lg_ref, qg_ref, kg_ref, ds_ref, ss_ref, gtot_ref = refs

    head = pl.program_id(1)
    ch = GLA_CHUNK
    n = t // ch
    gr = GLA_GROUP_ROWS
    cpg = gr // ch
    n_groups = t // gr
    unroll = min(n_groups, 8)
    nt = (((1,), (1,)), ((), ()))
    tn = (((0,), (0,)), ((), ()))

    lrb = lr_ref[...].astype(BF16)
    for d in range(2):
        zg = _mm(lrb, w2_ref[d]) + b2_ref[d]
        lg_ref[d] = (jnp.minimum(zg, 0.0) - jnp.log(1.0 + jnp.exp(-jnp.abs(zg)))) * (1.0 / GLA_GATE_NORM)

    def split3(x):
        hi = x.astype(BF16)
        rem = x - hi.astype(F32)
        mid = rem.astype(BF16)
        lo = (rem - mid.astype(F32)).astype(BF16)
        return hi, mid, lo

    row = lax.broadcasted_iota(jnp.int32, (gr, gr), 0)
    col = lax.broadcasted_iota(jnp.int32, (gr, gr), 1)
    same = (row // ch) == (col // ch)
    incl = (same & (col <= row), same & (col >= row))
    cum_mats = tuple(m.astype(BF16) for m in incl)
    brow = lax.broadcasted_iota(jnp.int32, (gr, cpg * GLA_DK), 0)
    bcol = lax.broadcasted_iota(jnp.int32, (gr, cpg * GLA_DK), 1)
    own_chunk = (brow // ch) == (bcol // GLA_DK)

    def spread(x):
        return jnp.where(own_chunk, jnp.concatenate([x] * cpg, axis=1), jnp.zeros((), x.dtype))

    if gtot_ref.shape[0] > n * 8:
        gtot_ref[...] = jnp.zeros_like(gtot_ref)

    for d in range(2):
        def prepare(g, carry, d=d):
            r0 = pl.multiple_of(g * gr, gr)
            lg3 = jnp.concatenate(split3(lg_ref[d, pl.ds(r0, gr), :]), axis=1)
            cums = _mm(cum_mats[d], lg3)
            gc = cums[:, 0:GLA_DK] + cums[:, GLA_DK:2 * GLA_DK] + cums[:, 2 * GLA_DK:3 * GLA_DK]
            g4 = gc.reshape(cpg, ch, GLA_DK)
            tot = g4[:, ch - 1:ch, :] if d == 0 else g4[:, 0:1, :]
            rc = (tot - g4).reshape(gr, GLA_DK)
            t0 = pl.multiple_of(g * (cpg * 8), cpg * 8)
            gtot_ref[pl.ds(t0, cpg * 8), :] = jnp.broadcast_to(tot, (cpg, 8, GLA_DK)).reshape(cpg * 8, GLA_DK)
            q = q_ref[pl.ds(r0, gr), :].astype(F32) * (GLA_DK ** -0.5)
            k = k_ref[pl.ds(r0, gr), :].astype(F32)
            qg_ref[pl.ds(r0, gr), :] = (q * jnp.exp(gc)).astype(BF16)
            kg_ref[pl.ds(r0, gr), :] = (k * jnp.exp(-gc)).astype(BF16)
            kd = (k * jnp.exp(rc)).astype(BF16)
            d0 = pl.multiple_of(g * (cpg * GLA_DK), cpg * GLA_DK)
            ds_ref[pl.ds(d0, cpg * GLA_DK), :] = lax.dot_general(
                spread(kd), v_ref[pl.ds(r0, gr), :], tn, preferred_element_type=F32)
            return carry

        lax.fori_loop(0, n_groups, prepare, 0, unroll=unroll)

        dec = jnp.exp(gtot_ref[...].T)
        cps = seq // ch
        first_c, last_c = (0, cps - 1) if d == 0 else (cps - 1, 0)
        for c in (range(n) if d == 0 else range(n - 1, -1, -1)):
            if c % cps == first_c:
                s = s0_ref[d] if has_s0 else jnp.zeros((GLA_DK, GLA_DV), F32)
            ss_ref[c * GLA_DK:(c + 1) * GLA_DK, :] = s.astype(BF16)
            s = s * dec[:, 8 * c:8 * c + 1] + ds_ref[c * GLA_DK:(c + 1) * GLA_DK, :]
            if want_state and c % cps == last_c:
                sfin_ref[c // cps, d] = s

        def attend(g, carry, d=d):
            r0 = pl.multiple_of(g * gr, gr)
            d0 = pl.multiple_of(g * (cpg * GLA_DK), cpg * GLA_DK)
            qg = qg_ref[pl.ds(r0, gr), :]
            a = lax.dot_general(qg, kg_ref[pl.ds(r0, gr), :], nt, preferred_element_type=F32)
            a = jnp.where(incl[d], a, 0.0).astype(BF16)
            o = _mm(a, v_ref[pl.ds(r0, gr), :]) + _mm(spread(qg), ss_ref[pl.ds(d0, cpg * GLA_DK), :])
            if d == 0:
                o_ref[pl.ds(r0, gr), :] = o
            else:
                o_ref[pl.ds(r0, gr), :] += o
            return carry

        lax.fori_loop(0, n_groups, attend, 0, unroll=unroll)

    o = o_ref[...]
    o = o * lax.rsqrt(jnp.mean(o * o, axis=-1, keepdims=True) + LN_EPS) * ng_ref[...]
    oall_ref[head] = (o * _silu(r_ref[...].astype(F32))).astype(BF16)

    @pl.when(head == pl.num_programs(1) - 1)
    def _():
        o_heads = jnp.concatenate([oall_ref[h] for h in range(GLA_H)], axis=1)
        gate = _sigmoid(gm_ref[...].astype(F32) + bm_ref[...])
        y_ref[...] = (gate * _mm(o_heads, wout_ref[...])).astype(BF16)


def _gla_call(z, lr, layer, b, seq, state, w2h, b2h, gla_norm_g, w_gla_out, b_merge, want_state):
    m = z.shape[0]
    has_s0 = state is not None
    t = seq if has_s0 else max(seq, min(GLA_BLOCK_ROWS, m))
    n_seq = t // seq
    assert t % seq == 0 and m % t == 0 and seq % GLA_GROUP_ROWS == 0
    n_chunks = t // GLA_CHUNK
    in_specs = [
        pl.BlockSpec((t, GLA_DK), lambda i, h: (i, COL_Q // GLA_DK + h)),
        pl.BlockSpec((t, GLA_DK), lambda i, h: (i, COL_K // GLA_DK + h)),
        pl.BlockSpec((t, GLA_DV), lambda i, h: (i, COL_VV // GLA_DV + h)),
        pl.BlockSpec((t, GLA_DV), lambda i, h: (i, COL_R // GLA_DV + h)),
        pl.BlockSpec((t, LR_PAD), lambda i, h: (i, 0)),
        pl.BlockSpec((None, None, 2, LR_PAD, GLA_DK), lambda i, h: (layer, h, 0, 0, 0)),
        pl.BlockSpec((None, None, 2, 1, GLA_DK), lambda i, h: (layer, h, 0, 0, 0)),
    ]
    args = [z, z, z, z, lr, w2h, b2h]
    if has_s0:
        in_specs.append(pl.BlockSpec((None, None, 2, None, GLA_DK, GLA_DV), lambda i, h: (i, layer, 0, h, 0, 0)))
        args.append(state)
    in_specs += [
        pl.BlockSpec((None, 1, GLA_DV), lambda i, h: (layer, 0, 0)),
        pl.BlockSpec((None, GLA_H * GLA_DV, D_MODEL), lambda i, h: (layer, 0, 0)),
        pl.BlockSpec((t, D_MODEL), lambda i, h: (i, COL_GM // D_MODEL + 2)),
        pl.BlockSpec((None, None, 1, D_MODEL), lambda i, h: (layer, 2, 0, 0)),
    ]
    args += [gla_norm_g, w_gla_out, z, b_merge]
    out_specs = [pl.BlockSpec((t, D_MODEL), lambda i, h: (i, 0))]
    out_shape = [jax.ShapeDtypeStruct((m, D_MODEL), BF16)]
    if want_state:
        out_specs.append(pl.BlockSpec((n_seq, 2, None, GLA_DK, GLA_DV), lambda i, h: (i, 0, h, 0, 0)))
        out_shape.append(jax.ShapeDtypeStruct((b, 2, GLA_H, GLA_DK, GLA_DV), F32))
    outs = pl.pallas_call(
        functools.partial(_gla_kernel, t=t, seq=seq, has_s0=has_s0, want_state=want_state),
        grid=(m // t, GLA_H),
        in_specs=in_specs,
        out_specs=out_specs,
        out_shape=out_shape,
        scratch_shapes=[
            pltpu.VMEM((GLA_H, t, GLA_DV), BF16),
            pltpu.VMEM((t, GLA_DV), F32),
            pltpu.VMEM((2, t, GLA_DK), F32),
            pltpu.VMEM((t, GLA_DK), BF16),
            pltpu.VMEM((t, GLA_DK), BF16),
            pltpu.VMEM((n_chunks * GLA_DK, GLA_DV), F32),
            pltpu.VMEM((n_chunks * GLA_DK, GLA_DV), BF16),
            pltpu.VMEM((max(n_chunks * 8, GLA_DK), GLA_DK), F32),
        ],
        compiler_params=_params(("arbitrary", "arbitrary")),
        name="gla",
    )(*args)
    return (outs[0], outs[1]) if want_state else (outs[0], None)


def _mix_kernel(ya_ref, yb_ref, yc_ref, x_ref, mod_ref, wo_ref, g_ref, b_ref, x1_ref, h2_ref, *, alpha):
    msum = (ya_ref[...].astype(F32) + yb_ref[...].astype(F32) + yc_ref[...].astype(F32)).astype(BF16)
    mix = _mm(msum, wo_ref[...])
    x1 = _norm(alpha * x_ref[...] + mod_ref[2:3, :] * mix) * g_ref[...] + b_ref[...]
    x1_ref[...] = x1
    h2_ref[...] = (_norm(x1) * (1.0 + mod_ref[4:5, :]) + mod_ref[3:4, :]).astype(BF16)


def _mix_call(ya, yb, yc, x, mod, layer, mod_row, w_o, ln1_g, ln1_b, alpha):
    m, d = x.shape
    tm = 512
    row = lambda i: (i, 0)
    vec = pl.BlockSpec((None, 1, d), lambda i: (layer, 0, 0))
    return pl.pallas_call(
        functools.partial(_mix_kernel, alpha=alpha),
        grid=(m // tm,),
        in_specs=[
            pl.BlockSpec((tm, d), row), pl.BlockSpec((tm, d), row), pl.BlockSpec((tm, d), row),
            pl.BlockSpec((tm, d), row),
            pl.BlockSpec((None, None, N_ADA, d), lambda i: (layer, mod_row((i * tm) // 1024), 0, 0)),
            pl.BlockSpec((None, d, d), lambda i: (layer, 0, 0)),
            vec, vec,
        ],
        out_specs=[pl.BlockSpec((tm, d), row), pl.BlockSpec((tm, d), row)],
        out_shape=[jax.ShapeDtypeStruct((m, d), F32), jax.ShapeDtypeStruct((m, d), BF16)],
        compiler_params=_params(("arbitrary",)),
        name="mix",
    )(ya, yb, yc, x, mod, w_o, ln1_g, ln1_b)


def _ffn_kernel(h2_ref, wa0_ref, wg0_ref, wa_ref, wg_ref, wdw_ref, bdw_ref, wd_ref, f_ref,
                a_even, g_even, a_odd, g_odd, wm_ref, *, seq, latent):
    k = pl.program_id(1)
    nk = pl.num_programs(1)
    n = pl.program_id(0) * nk + k

    rows, tc = g_even.shape
    cr = FFN_CHUNK_ROWS
    n_chunks = rows // cr
    period = GRID_W if latent else seq
    pad = GRID_W if latent else 0
    nl = cr // period

    def set_taps(wdw):
        pos = lax.broadcasted_iota(jnp.int32, (period, tc), 0)
        for i in (range(9) if latent else range(3, 6)):
            wi = jnp.broadcast_to(wdw[i:i + 1, :], (period, tc))
            if i % 3 == 0:
                wi = jnp.where(pos > 0, wi, 0.0)
            if i % 3 == 2:
                wi = jnp.where(pos < period - 1, wi, 0.0)
            wm_ref[i] = wi

    def up(r0, wa, wg, a_dst, g_dst):
        h2 = h2_ref[pl.ds(r0, cr), :]
        a_dst[pl.ds(r0 + pad, cr), :] = _mm(h2, wa[...])
        g_dst[pl.ds(r0, cr), :] = _mm(h2, wg[...])

    def conv_gate(r0, a_src, g_src):
        w = lambda i: wm_ref[i].reshape(1, period, tc)
        slab = a_src[pl.ds(r0, cr + 2 * pad), :].reshape((cr + 2 * pad) // period, period, tc)
        left = pltpu.roll(slab, 1, axis=1)
        right = pltpu.roll(slab, period - 1, axis=1)
        if latent:
            conv = sum(w(3 * dh) * left[dh:dh + nl] + w(3 * dh + 1) * slab[dh:dh + nl]
                       + w(3 * dh + 2) * right[dh:dh + nl] for dh in range(3))
        else:
            conv = w(3) * left + w(4) * slab + w(5) * right
        conv = conv.reshape(cr, tc) + bdw_ref[...]
        return (_gelu(conv) * g_src[pl.ds(r0, cr), :]).astype(BF16)

    def chunks(body, unroll):
        lax.fori_loop(0, n_chunks, lambda c, carry: (body(pl.multiple_of(c * cr, cr)), carry)[1], 0,
                      unroll=unroll)

    @pl.when(n == 0)
    def _():
        if pad:
            for a_ref in (a_even, a_odd):
                a_ref[0:pad, :] = jnp.zeros((pad, tc), F32)
                a_ref[pad + rows:pad + rows + pad, :] = jnp.zeros((pad, tc), F32)
        chunks(lambda r0: up(r0, wa0_ref, wg0_ref, a_even, g_even), 2)

    @pl.when(k == 0)
    def _():
        f_ref[...] = jnp.zeros_like(f_ref)

    def step(a_cur, g_cur, a_nxt, g_nxt):
        set_taps(wdw_ref)

        def body(r0):
            f_ref[pl.ds(r0, cr), :] += _mm(conv_gate(r0, a_cur, g_cur), wd_ref[...])
            up(r0, wa_ref, wg_ref, a_nxt, g_nxt)

        chunks(body, 8)

    @pl.when(n % 2 == 0)
    def _():
        step(a_even, g_even, a_odd, g_odd)

    @pl.when(n % 2 == 1)
    def _():
        step(a_odd, g_odd, a_even, g_even)


def _ffn_call(h2, layer, seq, latent, w_up, w_dw, b_dw, w_down):
    m, d = h2.shape
    rows = 2048
    tc = FFN_TILE_COLS
    nk = D_FF // tc
    nb = m // rows
    if latent:
        assert rows == seq and FFN_CHUNK_ROWS % GRID_W == 0
    else:
        assert FFN_CHUNK_ROWS % seq == 0
    pad = GRID_W if latent else 0
    period = GRID_W if latent else seq
    up_block = lambda i, k: jnp.minimum((i * nk + k + 1) // nk, nb - 1)
    up_tile = lambda k: (k + 1) % nk
    return pl.pallas_call(
        functools.partial(_ffn_kernel, seq=seq, latent=latent),
        grid=(nb, nk),
        in_specs=[
            pl.BlockSpec((rows, d), lambda i, k: (up_block(i, k), 0)),
            pl.BlockSpec((None, d, tc), lambda i, k: (layer, 0, 0)),
            pl.BlockSpec((None, d, tc), lambda i, k: (layer, 0, nk)),
            pl.BlockSpec((None, d, tc), lambda i, k: (layer, 0, up_tile(k))),
            pl.BlockSpec((None, d, tc), lambda i, k: (layer, 0, nk + up_tile(k))),
            pl.BlockSpec((None, 9, tc), lambda i, k: (layer, 0, k)),
            pl.BlockSpec((None, 1, tc), lambda i, k: (layer, 0, k)),
            pl.BlockSpec((None, tc, d), lambda i, k: (layer, k, 0)),
        ],
        out_specs=pl.BlockSpec((rows, d), lambda i, k: (i, 0)),
        out_shape=jax.ShapeDtypeStruct((m, d), F32),
        scratch_shapes=[pltpu.VMEM((rows + 2 * pad, tc), F32), pltpu.VMEM((rows, tc), F32)] * 2
        + [pltpu.VMEM((9, period, tc), F32)],
        compiler_params=_params(("arbitrary", "arbitrary")),
        name="ffn",
    )(h2, w_up, w_up, w_up, w_up, w_dw, b_dw, w_down)


def _ln2_kernel(x_ref, f_ref, mod_ref, g_ref, b_ref, o_ref, *, alpha):
    o_ref[...] = _norm(alpha * x_ref[...] + mod_ref[5:6, :] * f_ref[...]) * g_ref[...] + b_ref[...]


def _ln2_call(x1, f, mod, layer, mod_row, ln2_g, ln2_b, alpha):
    m, d = x1.shape
    tm = 1024
    row = lambda i: (i, 0)
    vec = pl.BlockSpec((None, 1, d), lambda i: (layer, 0, 0))
    return pl.pallas_call(
        functools.partial(_ln2_kernel, alpha=alpha),
        grid=(m // tm,),
        in_specs=[
            pl.BlockSpec((tm, d), row), pl.BlockSpec((tm, d), row),
            pl.BlockSpec((None, None, N_ADA, d), lambda i: (layer, mod_row(i), 0, 0)),
            vec, vec,
        ],
        out_specs=pl.BlockSpec((tm, d), row),
        out_shape=jax.ShapeDtypeStruct((m, d), F32),
        compiler_params=_params(("arbitrary",)),
        name="ln2",
    )(x1, f, mod, ln2_g, ln2_b)


def kernel(x_prompt, x_sample, state_gla, c, c_ctx, w_in, b_merge, w_fft_out, sgu_g, sgu_ws, sgu_b, w_sgu_out, gla_w2, gla_b, gla_norm_g, w_gla_out, w_o, ln1_g, ln1_b, w_ada, b_ada, w_up, w_dw, b_dw, w_down, ln2_g, ln2_b):
    depth = w_in.shape[0]
    alpha = (2.0 * depth) ** 0.25
    bp, tp, d = x_prompt.shape
    bs, ts, _ = x_sample.shape

    n_cond = 16
    cond = jnp.zeros((n_cond, d), F32).at[:bs].set(c.astype(F32)).at[bs].set(c_ctx.astype(F32))
    mod = _ada_call(cond, w_ada, b_ada).reshape(depth, n_cond, N_ADA, d)

    cuts = np.cumsum([FFT_W, SGU_W, SGU_W, GLA_H * GLA_DK, GLA_H * GLA_DK, GLA_H * GLA_DV, GLA_H * GLA_DV, 2 * GLA_LR]).tolist()
    p_af, p_u, p_v, p_q, p_k, p_vv, p_r, p_lr, p_gm = jnp.split(w_in, cuts, axis=-1)
    w_main = jnp.concatenate([p_vv, p_r, p_gm, p_af, p_u, p_v, p_q, p_k], axis=-1).astype(BF16)
    w_lr = jnp.pad(p_lr, ((0, 0), (0, 0), (0, LR_PAD - 2 * GLA_LR))).astype(BF16)
    w2 = gla_w2.reshape(depth, 2, GLA_LR, GLA_H, GLA_DK).transpose(0, 3, 1, 2, 4)
    w2h = jnp.zeros((depth, GLA_H, 2, LR_PAD, GLA_DK), F32)
    w2h = w2h.at[:, :, 0, 0:GLA_LR].set(w2[:, :, 0]).at[:, :, 1, GLA_LR:2 * GLA_LR].set(w2[:, :, 1]).astype(BF16)
    b2h = gla_b.reshape(depth, 2, GLA_H, 1, GLA_DK).transpose(0, 2, 1, 3, 4).astype(F32)
    sgu_bias = jnp.repeat(sgu_b.transpose(0, 2, 1), SGU_GC, axis=-1).astype(F32)
    vec = lambda a: a.reshape(depth, 1, a.shape[-1]).astype(F32)
    bm = b_merge.reshape(depth, N_BRANCH, 1, d).astype(F32)
    w_fft_out_b, w_sgu_out_b, sgu_ws_b = w_fft_out.astype(BF16), w_sgu_out.astype(BF16), sgu_ws.astype(BF16)
    w_gla_out_b, w_o_b, w_up_b, w_down_b = w_gla_out.astype(BF16), w_o.astype(BF16), w_up.astype(BF16), w_down.astype(BF16)
    w_dw9 = w_dw.reshape(depth, 9, D_FF).astype(F32)
    sgu_g3, ng3, ln1g, ln1b, ln2g, ln2b, bdw = (vec(a) for a in (sgu_g, gla_norm_g, ln1_g, ln1_b, ln2_g, ln2_b, b_dw))

    streams = [
        dict(x=x_prompt.reshape(bp * tp, d), b=bp, t=tp, latent=False, state=None, mod_row=lambda i: bs),
        dict(x=x_sample.reshape(bs * ts, d), b=bs, t=ts, latent=True, state=state_gla.astype(F32),
             mod_row=lambda i: (i * 1024) // ts),
    ]
    new_states = []
    for s in streams:
        x, f = s["x"].astype(F32), None
        tables = tuple(jnp.asarray(a).astype(BF16) for a in _dft_tables(s["t"]))
        for l in range(depth):
            prev_ln = (ln2g, ln2b) if l else None
            x2, z, lr = _in_call(x, f, mod, l, s["mod_row"], prev_ln, w_main, w_lr, alpha)
            x = x2 if l else x
            ya = _fft_call(z, l, s["b"], s["t"], tables, w_fft_out_b, bm)
            yb = _sgu_call(z, l, sgu_g3, sgu_ws_b, sgu_bias, w_sgu_out_b, bm)
            yc, st = _gla_call(z, lr, l, s["b"], s["t"], s["state"], w2h, b2h, ng3, w_gla_out_b, bm,
                               want_state=not s["latent"])
            if st is not None:
                new_states.append(st)
            x, h2 = _mix_call(ya, yb, yc, x, mod, l, s["mod_row"], w_o_b, ln1g, ln1b, alpha)
            f = _ffn_call(h2, l, s["t"], s["latent"], w_up_b, w_dw9, bdw, w_down_b)
        s["y"] = _ln2_call(x, f, mod, depth - 1, s["mod_row"], ln2g, ln2b, alpha)

    y_prompt = streams[0]["y"].reshape(bp, tp, d).astype(x_prompt.dtype)
    y_sample = streams[1]["y"].reshape(bs, ts, d).astype(x_sample.dtype)
    new_state_gla = jnp.stack(new_states, axis=1).astype(x_prompt.dtype)
    return (y_prompt, y_sample, new_state_gla)
```

```python
import functools
import math

import numpy as np
import jax
import jax.numpy as jnp
from jax import lax
from jax.experimental import pallas as pl
from jax.experimental.pallas import tpu as pltpu

F32 = jnp.float32
BF16 = jnp.bfloat16

D_MODEL = 1024
GRID_W = 64
FFT_GROUPS = 4
FFT_GC = 128
FFT_W = FFT_GROUPS * FFT_GC
SGU_GROUPS = 4
SGU_GC = 128
SGU_W = SGU_GROUPS * SGU_GC
SGU_CHUNK = 128
GLA_H = 4
GLA_DK = 128
GLA_DV = 256
GLA_LR = 16
GLA_GATE_NORM = 16.0
GLA_CHUNK = 64
FFT_FLIP_ROWS = 256
FFT_OUT_ROWS = 512
GLA_BLOCK_ROWS = 2048
GLA_GROUP_ROWS = 256
D_FF = 2816
N_BRANCH = 3
N_ADA = 6
LN_EPS = 1e-5
LR_PAD = 128
IN_CHUNK_ROWS = 256
FFN_TILE_COLS = 256
FFN_CHUNK_ROWS = 512

COL_VV = 0
COL_R = COL_VV + GLA_H * GLA_DV
COL_GM = COL_R + GLA_H * GLA_DV
COL_AF = COL_GM + N_BRANCH * D_MODEL
COL_U = COL_AF + FFT_W
COL_V = COL_U + SGU_W
COL_Q = COL_V + SGU_W
COL_K = COL_Q + GLA_H * GLA_DK
Z_W = COL_K + GLA_H * GLA_DK
IN_TN = 1536

V7X_VMEM_BYTES = 64 * 1024 * 1024
VMEM_LIMIT = V7X_VMEM_BYTES - 8 * 1024 * 1024


def _mm(a, b):
    return jnp.dot(a, b, preferred_element_type=F32)


def _sigmoid(x):
    return 0.5 * jnp.tanh(0.5 * x) + 0.5


def _silu(x):
    return x * _sigmoid(x)


def _gelu(x):
    c = math.sqrt(2.0 / math.pi)
    half = 0.5 * x
    return half + half * jnp.tanh(x * (c + (c * 0.044715) * (x * x)))


def _norm(x):
    mu = jnp.mean(x, axis=-1, keepdims=True)
    xc = x - mu
    var = jnp.mean(xc * xc, axis=-1, keepdims=True)
    return xc * lax.rsqrt(var + LN_EPS)


def _params(sem, limit=VMEM_LIMIT, flags=None):
    return pltpu.CompilerParams(dimension_semantics=sem, vmem_limit_bytes=limit, flags=flags)


def _ada_kernel(c_ref, w_ref, b_ref, o_ref):
    s = _silu(c_ref[...]).astype(BF16)
    o_ref[...] = _mm(s, w_ref[...].astype(BF16)) + b_ref[...]


def _ada_call(cond, w_ada, b_ada):
    depth, d, n = w_ada.shape
    rows = cond.shape[0]
    tn = 1536
    return pl.pallas_call(
        _ada_kernel,
        grid=(depth, n // tn),
        in_specs=[
            pl.BlockSpec((rows, d), lambda l, j: (0, 0)),
            pl.BlockSpec((None, d, tn), lambda l, j: (l, 0, j)),
            pl.BlockSpec((None, 1, tn), lambda l, j: (l, 0, j)),
        ],
        out_specs=pl.BlockSpec((None, rows, tn), lambda l, j: (l, 0, j)),
        out_shape=jax.ShapeDtypeStruct((depth, rows, n), F32),
        compiler_params=_params(("arbitrary", "arbitrary")),
        name="ada",
    )(cond, w_ada, b_ada.reshape(depth, 1, n))


def _in_kernel(*refs, alpha, has_prev):
    if has_prev:
        (x_ref, f_ref, pmod_ref, pg_ref, pb_ref, mod_ref, w_ref, wlr_ref,
         z_ref, lr_ref, x2_ref, h_ref) = refs
    else:
        x_ref, mod_ref, w_ref, wlr_ref, z_ref, lr_ref, h_ref = refs

    @pl.when(pl.program_id(1) == 0)
    def _():
        cr = IN_CHUNK_ROWS
        for c in range(x_ref.shape[0] // cr):
            rows = slice(c * cr, (c + 1) * cr)
            x = x_ref[rows, :]
            if has_prev:
                g2 = pmod_ref[5:6, :]
                x = _norm(alpha * x + g2 * f_ref[rows, :]) * pg_ref[...] + pb_ref[...]
                x2_ref[rows, :] = x
            h = (_norm(x) * (1.0 + mod_ref[1:2, :]) + mod_ref[0:1, :]).astype(BF16)
            h_ref[rows, :] = h
            lr_ref[rows, :] = _mm(h, wlr_ref[...])
            z_ref[rows, :] = _mm(h, w_ref[...]).astype(BF16)

    @pl.when(pl.program_id(1) != 0)
    def _():
        z_ref[...] = _mm(h_ref[...], w_ref[...]).astype(BF16)


def _in_call(x, f, mod, layer, mod_row, prev_ln, w_main, w_lr, alpha):
    m, d = x.shape
    tm = 1024
    tn = IN_TN
    has_prev = f is not None
    row = lambda i, j: (i, 0)
    in_specs = [pl.BlockSpec((tm, d), row)]
    args = [x]
    if has_prev:
        pg, pb = prev_ln
        in_specs += [
            pl.BlockSpec((tm, d), row),
            pl.BlockSpec((None, None, N_ADA, d), lambda i, j: (layer - 1, mod_row(i), 0, 0)),
            pl.BlockSpec((None, 1, d), lambda i, j: (layer - 1, 0, 0)),
            pl.BlockSpec((None, 1, d), lambda i, j: (layer - 1, 0, 0)),
        ]
        args += [f, mod, pg, pb]
    in_specs += [
        pl.BlockSpec((None, None, N_ADA, d), lambda i, j: (layer, mod_row(i), 0, 0)),
        pl.BlockSpec((None, d, tn), lambda i, j: (layer, 0, j)),
        pl.BlockSpec((None, d, LR_PAD), lambda i, j: (layer, 0, 0)),
    ]
    args += [mod, w_main, w_lr]
    out_specs = [pl.BlockSpec((tm, tn), lambda i, j: (i, j)), pl.BlockSpec((tm, LR_PAD), row)]
    out_shape = [jax.ShapeDtypeStruct((m, Z_W), BF16), jax.ShapeDtypeStruct((m, LR_PAD), F32)]
    if has_prev:
        out_specs.append(pl.BlockSpec((tm, d), row))
        out_shape.append(jax.ShapeDtypeStruct((m, d), F32))
    outs = pl.pallas_call(
        functools.partial(_in_kernel, alpha=alpha, has_prev=has_prev),
        grid=(m // tm, Z_W // tn),
        in_specs=in_specs,
        out_specs=out_specs,
        out_shape=out_shape,
        scratch_shapes=[pltpu.VMEM((tm, d), BF16)],
        compiler_params=_params(("arbitrary", "arbitrary")),
        name="in_proj",
    )(*args)
    if has_prev:
        z, lr, x2 = outs
        return x2, z, lr
    z, lr = outs
    return None, z, lr


@functools.lru_cache(maxsize=None)
def _dft_tables(t):
    def cs(rows, n):
        ang = (2.0 * np.pi / n) * ((np.arange(rows)[:, None] * np.arange(n)[None, :]) % n).astype(np.float64)
        return np.cos(ang).astype(np.float32), np.sin(ang).astype(np.float32)

    ct, st = cs(t // 2 + 8, t)
    cc, sc = cs(FFT_GC, FFT_GC)
    flip = np.eye(min(FFT_FLIP_ROWS, t // 2), dtype=np.float32)[::-1].copy()
    return ct, st, cc, sc, flip


def _fft_kernel(a_ref, gm_ref, ct_ref, st_ref, cc_ref, sc_ref, flip_ref, w_ref, bm_ref, y_ref,
                xc_ref, xs_ref, af_ref, *, t):
    half = t // 2
    fb = flip_ref.shape[0]
    for g in range(FFT_GROUPS):
        cols = slice(g * FFT_GC, (g + 1) * FFT_GC)
        xg = a_ref[:, cols]
        xc_ref[:, cols] = _mm(xg, cc_ref[...]).astype(BF16)
        xs_ref[:, cols] = _mm(xg, sc_ref[...]).astype(BF16)
    scale = 1.0 / math.sqrt(t * FFT_GC)
    p = _mm(ct_ref[...], xc_ref[...]) * scale
    q = _mm(st_ref[...], xs_ref[...]) * scale
    af_ref[0:half, :] = (p - q)[0:half].astype(BF16)
    mirrored = (p + q)[1:half + 1].astype(BF16)
    for i in range(half // fb):
        block = mirrored[half - (i + 1) * fb:half - i * fb]
        af_ref[half + i * fb:half + (i + 1) * fb, :] = _mm(flip_ref[...], block).astype(BF16)
    tr = min(t, FFT_OUT_ROWS)
    for r in range(t // tr):
        rows = slice(r * tr, (r + 1) * tr)
        gate = _sigmoid(gm_ref[rows, :].astype(F32) + bm_ref[...])
        y_ref[rows, :] = (gate * _mm(af_ref[rows, :], w_ref[...])).astype(BF16)


def _fft_call(z, layer, b, t, tables, w_fft_out, b_merge):
    m = z.shape[0]
    ct, st, cc, sc, flip = tables
    const = lambda a: pl.BlockSpec(a.shape, lambda i: (0, 0), pipeline_mode=pl.Buffered(1))
    return pl.pallas_call(
        functools.partial(_fft_kernel, t=t),
        grid=(b,),
        in_specs=[
            pl.BlockSpec((t, FFT_W), lambda i: (i, COL_AF // FFT_W)),
            pl.BlockSpec((t, D_MODEL), lambda i: (i, COL_GM // D_MODEL)),
            const(ct), const(st), const(cc), const(sc), const(flip),
            pl.BlockSpec((None, FFT_W, D_MODEL), lambda i: (layer, 0, 0)),
            pl.BlockSpec((None, None, 1, D_MODEL), lambda i: (layer, 0, 0, 0)),
        ],
        out_specs=pl.BlockSpec((t, D_MODEL), lambda i: (i, 0)),
        out_shape=jax.ShapeDtypeStruct((m, D_MODEL), BF16),
        scratch_shapes=[pltpu.VMEM((t, FFT_W), BF16)] * 3,
        compiler_params=_params(("arbitrary",)),
        name="fft",
    )(z, z, ct, st, cc, sc, flip, w_fft_out, b_merge)


def _sgu_kernel(u_ref, v_ref, gm_ref, g_ref, ws_ref, bias_ref, w_ref, bm_ref, y_ref, vm_ref):
    tm = u_ref.shape[0]
    v = _gelu(v_ref[...].astype(F32))
    vn = (_norm(v) * g_ref[...]).astype(BF16)
    for n in range(tm // SGU_CHUNK):
        rows = slice(n * SGU_CHUNK, (n + 1) * SGU_CHUNK)
        for g in range(SGU_GROUPS):
            cols = slice(g * SGU_GC, (g + 1) * SGU_GC)
            vm_ref[rows, cols] = _mm(ws_ref[g], vn[rows, cols]) + bias_ref[:, cols]
    u = _gelu(u_ref[...].astype(F32))
    yb = _mm((u * vm_ref[...]).astype(BF16), w_ref[...])
    gate = _sigmoid(gm_ref[...].astype(F32) + bm_ref[...])
    y_ref[...] = (gate * yb).astype(BF16)


def _sgu_call(z, layer, sgu_g, sgu_ws, sgu_bias, w_sgu_out, b_merge):
    m = z.shape[0]
    tm = 1024
    return pl.pallas_call(
        _sgu_kernel,
        grid=(m // tm,),
        in_specs=[
            pl.BlockSpec((tm, SGU_W), lambda i: (i, COL_U // SGU_W)),
            pl.BlockSpec((tm, SGU_W), lambda i: (i, COL_V // SGU_W)),
            pl.BlockSpec((tm, D_MODEL), lambda i: (i, COL_GM // D_MODEL + 1)),
            pl.BlockSpec((None, 1, SGU_W), lambda i: (layer, 0, 0)),
            pl.BlockSpec((None, SGU_GROUPS, SGU_CHUNK, SGU_CHUNK), lambda i: (layer, 0, 0, 0)),
            pl.BlockSpec((None, SGU_CHUNK, SGU_W), lambda i: (layer, 0, 0)),
            pl.BlockSpec((None, SGU_W, D_MODEL), lambda i: (layer, 0, 0)),
            pl.BlockSpec((None, None, 1, D_MODEL), lambda i: (layer, 1, 0, 0)),
        ],
        out_specs=pl.BlockSpec((tm, D_MODEL), lambda i: (i, 0)),
        out_shape=jax.ShapeDtypeStruct((m, D_MODEL), BF16),
        scratch_shapes=[pltpu.VMEM((tm, SGU_W), F32)],
        compiler_params=_params(("arbitrary",)),
        name="sgu",
    )(z, z, z, sgu_g, sgu_ws, sgu_bias, w_sgu_out, b_merge)


def _gla_kernel(*refs, t, seq, has_s0, want_state):
    refs = list(refs)
    q_ref, k_ref, v_ref, r_ref, lr_ref, w2_ref, b2_ref = refs[:7]
    del refs[:7]
    s0_ref = refs.pop(0) if has_s0 else None
    ng_ref, wout_ref, gm_ref, bm_ref, y_ref = refs[:5]
    del refs[:5]
    sfin_ref = refs.pop(0) if want_state else None
    oall_ref, o_ref, lg_ref, qg_ref, kg_ref, ds_ref, ss_ref, gtot_ref = refs

    head = pl.program_id(1)
    ch = GLA_CHUNK
    n = t // ch
    gr = GLA_GROUP_ROWS
    cpg = gr // ch
    n_groups = t // gr
    unroll = min(n_groups, 8)
    nt = (((1,), (1,)), ((), ()))
    tn = (((0,), (0,)), ((), ()))

    lrb = lr_ref[...].astype(BF16)
    for d in range(2):
        zg = _mm(lrb, w2_ref[d]) + b2_ref[d]
        lg_ref[d] = (jnp.minimum(zg, 0.0) - jnp.log(1.0 + jnp.exp(-jnp.abs(zg)))) * (1.0 / GLA_GATE_NORM)

    def split3(x):
        hi = x.astype(BF16)
        rem = x - hi.astype(F32)
        mid = rem.astype(BF16)
        lo = (rem - mid.astype(F32)).astype(BF16)
        return hi, mid, lo

    row = lax.broadcasted_iota(jnp.int32, (gr, gr), 0)
    col = lax.broadcasted_iota(jnp.int32, (gr, gr), 1)
    same = (row // ch) == (col // ch)
    incl = (same & (col <= row), same & (col >= row))
    cum_mats = tuple(m.astype(BF16) for m in incl)
    brow = lax.broadcasted_iota(jnp.int32, (gr, cpg * GLA_DK), 0)
    bcol = lax.broadcasted_iota(jnp.int32, (gr, cpg * GLA_DK), 1)
    own_chunk = (brow // ch) == (bcol // GLA_DK)

    def spread(x):
        return jnp.where(own_chunk, jnp.concatenate([x] * cpg, axis=1), jnp.zeros((), x.dtype))

    if gtot_ref.shape[0] > n * 8:
        gtot_ref[...] = jnp.zeros_like(gtot_ref)

    for d in range(2):
        def prepare(g, carry, d=d):
            r0 = pl.multiple_of(g * gr, gr)
            lg3 = jnp.concatenate(split3(lg_ref[d, pl.ds(r0, gr), :]), axis=1)
            cums = _mm(cum_mats[d], lg3)
            gc = cums[:, 0:GLA_DK] + cums[:, GLA_DK:2 * GLA_DK] + cums[:, 2 * GLA_DK:3 * GLA_DK]
            g4 = gc.reshape(cpg, ch, GLA_DK)
            tot = g4[:, ch - 1:ch, :] if d == 0 else g4[:, 0:1, :]
            rc = (tot - g4).reshape(gr, GLA_DK)
            t0 = pl.multiple_of(g * (cpg * 8), cpg * 8)
            gtot_ref[pl.ds(t0, cpg * 8), :] = jnp.broadcast_to(tot, (cpg, 8, GLA_DK)).reshape(cpg * 8, GLA_DK)
            q = q_ref[pl.ds(r0, gr), :].astype(F32) * (GLA_DK ** -0.5)
            k = k_ref[pl.ds(r0, gr), :].astype(F32)
            qg_ref[pl.ds(r0, gr), :] = (q * jnp.exp(gc)).astype(BF16)
            kg_ref[pl.ds(r0, gr), :] = (k * jnp.exp(-gc)).astype(BF16)
            kd = (k * jnp.exp(rc)).astype(BF16)
            d0 = pl.multiple_of(g * (cpg * GLA_DK), cpg * GLA_DK)
            ds_ref[pl.ds(d0, cpg * GLA_DK), :] = lax.dot_general(
                spread(kd), v_ref[pl.ds(r0, gr), :], tn, preferred_element_type=F32)
            return carry

        lax.fori_loop(0, n_groups, prepare, 0, unroll=unroll)

        dec = jnp.exp(gtot_ref[...].T)
        cps = seq // ch
        first_c, last_c = (0, cps - 1) if d == 0 else (cps - 1, 0)
        for c in (range(n) if d == 0 else range(n - 1, -1, -1)):
            if c % cps == first_c:
                s = s0_ref[d] if has_s0 else jnp.zeros((GLA_DK, GLA_DV), F32)
            ss_ref[c * GLA_DK:(c + 1) * GLA_DK, :] = s.astype(BF16)
            s = s * dec[:, 8 * c:8 * c + 1] + ds_ref[c * GLA_DK:(c + 1) * GLA_DK, :]
            if want_state and c % cps == last_c:
                sfin_ref[c // cps, d] = s

        def attend(g, carry, d=d):
            r0 = pl.multiple_of(g * gr, gr)
            d0 = pl.multiple_of(g * (cpg * GLA_DK), cpg * GLA_DK)
            qg = qg_ref[pl.ds(r0, gr), :]
            a = lax.dot_general(qg, kg_ref[pl.ds(r0, gr), :], nt, preferred_element_type=F32)
            a = jnp.where(incl[d], a, 0.0).astype(BF16)
            o = _mm(a, v_ref[pl.ds(r0, gr), :]) + _mm(spread(qg), ss_ref[pl.ds(d0, cpg * GLA_DK), :])
            if d == 0:
                o_ref[pl.ds(r0, gr), :] = o
            else:
                o_ref[pl.ds(r0, gr), :] += o
            return carry

        lax.fori_loop(0, n_groups, attend, 0, unroll=unroll)

    o = o_ref[...]
    o = o * lax.rsqrt(jnp.mean(o * o, axis=-1, keepdims=True) + LN_EPS) * ng_ref[...]
    oall_ref[head] = (o * _silu(r_ref[...].astype(F32))).astype(BF16)

    @pl.when(head == pl.num_programs(1) - 1)
    def _():
        o_heads = jnp.concatenate([oall_ref[h] for h in range(GLA_H)], axis=1)
        gate = _sigmoid(gm_ref[...].astype(F32) + bm_ref[...])
        y_ref[...] = (gate * _mm(o_heads, wout_ref[...])).astype(BF16)


def _gla_call(z, lr, layer, b, seq, state, w2h, b2h, gla_norm_g, w_gla_out, b_merge, want_state):
    m = z.shape[0]
    has_s0 = state is not None
    t = seq if has_s0 else max(seq, min(GLA_BLOCK_ROWS, m))
    n_seq = t // seq
    assert t % seq == 0 and m % t == 0 and seq % GLA_GROUP_ROWS == 0
    n_chunks = t // GLA_CHUNK
    in_specs = [
        pl.BlockSpec((t, GLA_DK), lambda i, h: (i, COL_Q // GLA_DK + h)),
        pl.BlockSpec((t, GLA_DK), lambda i, h: (i, COL_K // GLA_DK + h)),
        pl.BlockSpec((t, GLA_DV), lambda i, h: (i, COL_VV // GLA_DV + h)),
        pl.BlockSpec((t, GLA_DV), lambda i, h: (i, COL_R // GLA_DV + h)),
        pl.BlockSpec((t, LR_PAD), lambda i, h: (i, 0)),
        pl.BlockSpec((None, None, 2, LR_PAD, GLA_DK), lambda i, h: (layer, h, 0, 0, 0)),
        pl.BlockSpec((None, None, 2, 1, GLA_DK), lambda i, h: (layer, h, 0, 0, 0)),
    ]
    args = [z, z, z, z, lr, w2h, b2h]
    if has_s0:
        in_specs.append(pl.BlockSpec((None, None, 2, None, GLA_DK, GLA_DV), lambda i, h: (i, layer, 0, h, 0, 0)))
        args.append(state)
    in_specs += [
        pl.BlockSpec((None, 1, GLA_DV), lambda i, h: (layer, 0, 0)),
        pl.BlockSpec((None, GLA_H * GLA_DV, D_MODEL), lambda i, h: (layer, 0, 0)),
        pl.BlockSpec((t, D_MODEL), lambda i, h: (i, COL_GM // D_MODEL + 2)),
        pl.BlockSpec((None, None, 1, D_MODEL), lambda i, h: (layer, 2, 0, 0)),
    ]
    args += [gla_norm_g, w_gla_out, z, b_merge]
    out_specs = [pl.BlockSpec((t, D_MODEL), lambda i, h: (i, 0))]
    out_shape = [jax.ShapeDtypeStruct((m, D_MODEL), BF16)]
    if want_state:
        out_specs.append(pl.BlockSpec((n_seq, 2, None, GLA_DK, GLA_DV), lambda i, h: (i, 0, h, 0, 0)))
        out_shape.append(jax.ShapeDtypeStruct((b, 2, GLA_H, GLA_DK, GLA_DV), F32))
    outs = pl.pallas_call(
        functools.partial(_gla_kernel, t=t, seq=seq, has_s0=has_s0, want_state=want_state),
        grid=(m // t, GLA_H),
        in_specs=in_specs,
        out_specs=out_specs,
        out_shape=out_shape,
        scratch_shapes=[
            pltpu.VMEM((GLA_H, t, GLA_DV), BF16),
            pltpu.VMEM((t, GLA_DV), F32),
            pltpu.VMEM((2, t, GLA_DK), F32),
            pltpu.VMEM((t, GLA_DK), BF16),
            pltpu.VMEM((t, GLA_DK), BF16),
            pltpu.VMEM((n_chunks * GLA_DK, GLA_DV), F32),
            pltpu.VMEM((n_chunks * GLA_DK, GLA_DV), BF16),
            pltpu.VMEM((max(n_chunks * 8, GLA_DK), GLA_DK), F32),
        ],
        compiler_params=_params(("arbitrary", "arbitrary")),
        name="gla",
    )(*args)
    return (outs[0], outs[1]) if want_state else (outs[0], None)


def _mix_kernel(ya_ref, yb_ref, yc_ref, x_ref, mod_ref, wo_ref, g_ref, b_ref, x1_ref, h2_ref, *, alpha):
    msum = (ya_ref[...].astype(F32) + yb_ref[...].astype(F32) + yc_ref[...].astype(F32)).astype(BF16)
    mix = _mm(msum, wo_ref[...])
    x1 = _norm(alpha * x_ref[...] + mod_ref[2:3, :] * mix) * g_ref[...] + b_ref[...]
    x1_ref[...] = x1
    h2_ref[...] = (_norm(x1) * (1.0 + mod_ref[4:5, :]) + mod_ref[3:4, :]).astype(BF16)


def _mix_call(ya, yb, yc, x, mod, layer, mod_row, w_o, ln1_g, ln1_b, alpha):
    m, d = x.shape
    tm = 512
    row = lambda i: (i, 0)
    vec = pl.BlockSpec((None, 1, d), lambda i: (layer, 0, 0))
    return pl.pallas_call(
        functools.partial(_mix_kernel, alpha=alpha),
        grid=(m // tm,),
        in_specs=[
            pl.BlockSpec((tm, d), row), pl.BlockSpec((tm, d), row), pl.BlockSpec((tm, d), row),
            pl.BlockSpec((tm, d), row),
            pl.BlockSpec((None, None, N_ADA, d), lambda i: (layer, mod_row((i * tm) // 1024), 0, 0)),
            pl.BlockSpec((None, d, d), lambda i: (layer, 0, 0)),
            vec, vec,
        ],
        out_specs=[pl.BlockSpec((tm, d), row), pl.BlockSpec((tm, d), row)],
        out_shape=[jax.ShapeDtypeStruct((m, d), F32), jax.ShapeDtypeStruct((m, d), BF16)],
        compiler_params=_params(("arbitrary",)),
        name="mix",
    )(ya, yb, yc, x, mod, w_o, ln1_g, ln1_b)


def _ffn_kernel(h2_ref, wa0_ref, wg0_ref, wa_ref, wg_ref, wdw_ref, bdw_ref, wd_ref, f_ref,
                a_even, g_even, a_odd, g_odd, wm_ref, *, seq, latent):
    k = pl.program_id(1)
    nk = pl.num_programs(1)
    n = pl.program_id(0) * nk + k

    rows, tc = g_even.shape
    cr = FFN_CHUNK_ROWS
    n_chunks = rows // cr
    period = GRID_W if latent else seq
    pad = GRID_W if latent else 0
    nl = cr // period

    def set_taps(wdw):
        pos = lax.broadcasted_iota(jnp.int32, (period, tc), 0)
        for i in (range(9) if latent else range(3, 6)):
            wi = jnp.broadcast_to(wdw[i:i + 1, :], (period, tc))
            if i % 3 == 0:
                wi = jnp.where(pos > 0, wi, 0.0)
            if i % 3 == 2:
                wi = jnp.where(pos < period - 1, wi, 0.0)
            wm_ref[i] = wi

    def up(r0, wa, wg, a_dst, g_dst):
        h2 = h2_ref[pl.ds(r0, cr), :]
        a_dst[pl.ds(r0 + pad, cr), :] = _mm(h2, wa[...])
        g_dst[pl.ds(r0, cr), :] = _mm(h2, wg[...])

    def conv_gate(r0, a_src, g_src):
        w = lambda i: wm_ref[i].reshape(1, period, tc)
        slab = a_src[pl.ds(r0, cr + 2 * pad), :].reshape((cr + 2 * pad) // period, period, tc)
        if latent:
            raw = lambda i: wdw_ref[i:i + 1, :].reshape(1, 1, tc)
            col = lambda dw: sum(raw(3 * dh + dw) * slab[dh:dh + nl] for dh in range(3))
            pos = lax.broadcasted_iota(jnp.int32, (1, period, tc), 1)
            conv = (jnp.where(pos > 0, pltpu.roll(col(0), 1, axis=1), 0.0) + col(1)
                    + jnp.where(pos < period - 1, pltpu.roll(col(2), period - 1, axis=1), 0.0))
        else:
            left = pltpu.roll(slab, 1, axis=1)
            right = pltpu.roll(slab, period - 1, axis=1)
            conv = w(3) * left + w(4) * slab + w(5) * right
        conv = conv.reshape(cr, tc) + bdw_ref[...]
        return (_gelu(conv) * g_src[pl.ds(r0, cr), :]).astype(BF16)

    def chunks(body, unroll):
        lax.fori_loop(0, n_chunks, lambda c, carry: (body(pl.multiple_of(c * cr, cr)), carry)[1], 0,
                      unroll=unroll)

    @pl.when(n == 0)
    def _():
        if pad:
            for a_ref in (a_even, a_odd):
                a_ref[0:pad, :] = jnp.zeros((pad, tc), F32)
                a_ref[pad + rows:pad + rows + pad, :] = jnp.zeros((pad, tc), F32)
        chunks(lambda r0: up(r0, wa0_ref, wg0_ref, a_even, g_even), 2)

    @pl.when(k == 0)
    def _():
        f_ref[...] = jnp.zeros_like(f_ref)

    def step(a_cur, g_cur, a_nxt, g_nxt):
        if not latent:
            set_taps(wdw_ref)

        def body(r0):
            f_ref[pl.ds(r0, cr), :] += _mm(conv_gate(r0, a_cur, g_cur), wd_ref[...])
            up(r0, wa_ref, wg_ref, a_nxt, g_nxt)

        chunks(body, 8)

    @pl.when(n % 2 == 0)
    def _():
        step(a_even, g_even, a_odd, g_odd)

    @pl.when(n % 2 == 1)
    def _():
        step(a_odd, g_odd, a_even, g_even)


def _ffn_call(h2, layer, seq, latent, w_up, w_dw, b_dw, w_down):
    m, d = h2.shape
    rows = 2048
    tc = FFN_TILE_COLS
    nk = D_FF // tc
    nb = m // rows
    if latent:
        assert rows == seq and FFN_CHUNK_ROWS % GRID_W == 0
    else:
        assert FFN_CHUNK_ROWS % seq == 0
    pad = GRID_W if latent else 0
    period = GRID_W if latent else seq
    up_block = lambda i, k: jnp.minimum((i * nk + k + 1) // nk, nb - 1)
    up_tile = lambda k: (k + 1) % nk
    return pl.pallas_call(
        functools.partial(_ffn_kernel, seq=seq, latent=latent),
        grid=(nb, nk),
        in_specs=[
            pl.BlockSpec((rows, d), lambda i, k: (up_block(i, k), 0)),
            pl.BlockSpec((None, d, tc), lambda i, k: (layer, 0, 0)),
            pl.BlockSpec((None, d, tc), lambda i, k: (layer, 0, nk)),
            pl.BlockSpec((None, d, tc), lambda i, k: (layer, 0, up_tile(k))),
            pl.BlockSpec((None, d, tc), lambda i, k: (layer, 0, nk + up_tile(k))),
            pl.BlockSpec((None, 9, tc), lambda i, k: (layer, 0, k)),
            pl.BlockSpec((None, 1, tc), lambda i, k: (layer, 0, k)),
            pl.BlockSpec((None, tc, d), lambda i, k: (layer, k, 0)),
        ],
        out_specs=pl.BlockSpec((rows, d), lambda i, k: (i, 0)),
        out_shape=jax.ShapeDtypeStruct((m, d), F32),
        scratch_shapes=[pltpu.VMEM((rows + 2 * pad, tc), F32), pltpu.VMEM((rows, tc), F32)] * 2
        + [pltpu.VMEM((9, period, tc), F32)],
        compiler_params=_params(("arbitrary", "arbitrary")),
        name="ffn",
    )(h2, w_up, w_up, w_up, w_up, w_dw, b_dw, w_down)


def _ln2_kernel(x_ref, f_ref, mod_ref, g_ref, b_ref, o_ref, *, alpha):
    o_ref[...] = _norm(alpha * x_ref[...] + mod_ref[5:6, :] * f_ref[...]) * g_ref[...] + b_ref[...]


def _ln2_call(x1, f, mod, layer, mod_row, ln2_g, ln2_b, alpha):
    m, d = x1.shape
    tm = 1024
    row = lambda i: (i, 0)
    vec = pl.BlockSpec((None, 1, d), lambda i: (layer, 0, 0))
    return pl.pallas_call(
        functools.partial(_ln2_kernel, alpha=alpha),
        grid=(m // tm,),
        in_specs=[
            pl.BlockSpec((tm, d), row), pl.BlockSpec((tm, d), row),
            pl.BlockSpec((None, None, N_ADA, d), lambda i: (layer, mod_row(i), 0, 0)),
            vec, vec,
        ],
        out_specs=pl.BlockSpec((tm, d), row),
        out_shape=jax.ShapeDtypeStruct((m, d), F32),
        compiler_params=_params(("arbitrary",)),
        name="ln2",
    )(x1, f, mod, ln2_g, ln2_b)


def kernel(x_prompt, x_sample, state_gla, c, c_ctx, w_in, b_merge, w_fft_out, sgu_g, sgu_ws, sgu_b, w_sgu_out, gla_w2, gla_b, gla_norm_g, w_gla_out, w_o, ln1_g, ln1_b, w_ada, b_ada, w_up, w_dw, b_dw, w_down, ln2_g, ln2_b):
    depth = w_in.shape[0]
    alpha = (2.0 * depth) ** 0.25
    bp, tp, d = x_prompt.shape
    bs, ts, _ = x_sample.shape

    n_cond = 16
    cond = jnp.zeros((n_cond, d), F32).at[:bs].set(c.astype(F32)).at[bs].set(c_ctx.astype(F32))
    mod = _ada_call(cond, w_ada, b_ada).reshape(depth, n_cond, N_ADA, d)

    cuts = np.cumsum([FFT_W, SGU_W, SGU_W, GLA_H * GLA_DK, GLA_H * GLA_DK, GLA_H * GLA_DV, GLA_H * GLA_DV, 2 * GLA_LR]).tolist()
    p_af, p_u, p_v, p_q, p_k, p_vv, p_r, p_lr, p_gm = jnp.split(w_in, cuts, axis=-1)
    w_main = jnp.concatenate([p_vv, p_r, p_gm, p_af, p_u, p_v, p_q, p_k], axis=-1).astype(BF16)
    w_lr = jnp.pad(p_lr, ((0, 0), (0, 0), (0, LR_PAD - 2 * GLA_LR))).astype(BF16)
    w2 = gla_w2.reshape(depth, 2, GLA_LR, GLA_H, GLA_DK).transpose(0, 3, 1, 2, 4)
    w2h = jnp.zeros((depth, GLA_H, 2, LR_PAD, GLA_DK), F32)
    w2h = w2h.at[:, :, 0, 0:GLA_LR].set(w2[:, :, 0]).at[:, :, 1, GLA_LR:2 * GLA_LR].set(w2[:, :, 1]).astype(BF16)
    b2h = gla_b.reshape(depth, 2, GLA_H, 1, GLA_DK).transpose(0, 2, 1, 3, 4).astype(F32)
    sgu_bias = jnp.repeat(sgu_b.transpose(0, 2, 1), SGU_GC, axis=-1).astype(F32)
    vec = lambda a: a.reshape(depth, 1, a.shape[-1]).astype(F32)
    bm = b_merge.reshape(depth, N_BRANCH, 1, d).astype(F32)
    w_fft_out_b, w_sgu_out_b, sgu_ws_b = w_fft_out.astype(BF16), w_sgu_out.astype(BF16), sgu_ws.astype(BF16)
    w_gla_out_b, w_o_b, w_up_b, w_down_b = w_gla_out.astype(BF16), w_o.astype(BF16), w_up.astype(BF16), w_down.astype(BF16)
    w_dw9 = w_dw.reshape(depth, 9, D_FF).astype(F32)
    sgu_g3, ng3, ln1g, ln1b, ln2g, ln2b, bdw = (vec(a) for a in (sgu_g, gla_norm_g, ln1_g, ln1_b, ln2_g, ln2_b, b_dw))

    streams = [
        dict(x=x_prompt.reshape(bp * tp, d), b=bp, t=tp, latent=False, state=None, mod_row=lambda i: bs),
        dict(x=x_sample.reshape(bs * ts, d), b=bs, t=ts, latent=True, state=state_gla.astype(F32),
             mod_row=lambda i: (i * 1024) // ts),
    ]
    new_states = []
    for s in streams:
        x, f = s["x"].astype(F32), None
        tables = tuple(jnp.asarray(a).astype(BF16) for a in _dft_tables(s["t"]))
        for l in range(depth):
            prev_ln = (ln2g, ln2b) if l else None
            x2, z, lr = _in_call(x, f, mod, l, s["mod_row"], prev_ln, w_main, w_lr, alpha)
            x = x2 if l else x
            ya = _fft_call(z, l, s["b"], s["t"], tables, w_fft_out_b, bm)
            yb = _sgu_call(z, l, sgu_g3, sgu_ws_b, sgu_bias, w_sgu_out_b, bm)
            yc, st = _gla_call(z, lr, l, s["b"], s["t"], s["state"], w2h, b2h, ng3, w_gla_out_b, bm,
                               want_state=not s["latent"])
            if st is not None:
                new_states.append(st)
            x, h2 = _mix_call(ya, yb, yc, x, mod, l, s["mod_row"], w_o_b, ln1g, ln1b, alpha)
            f = _ffn_call(h2, l, s["t"], s["latent"], w_up_b, w_dw9, bdw, w_down_b)
        s["y"] = _ln2_call(x, f, mod, depth - 1, s["mod_row"], ln2g, ln2b, alpha)

    y_prompt = streams[0]["y"].reshape(bp, tp, d).astype(x_prompt.dtype)
    y_sample = streams[1]["y"].reshape(bs, ts, d).astype(x_sample.dtype)
    new_state_gla = jnp.stack(new_states, axis=1).astype(x_prompt.dtype)
    return (y_prompt, y_sample, new_state_gla)
```
